```python
import math
import jax, jax.numpy as jnp
from jax import lax
import numpy as np

D_MODEL = 1024
BATCH = 8
SEQ = 2048
DEPTH = 4
DEC_BATCH = 2
DEC_SEQ = 16384
PAST_LEN = 128

HEAD_DIM = 64
N_MIXERS = 3
GQA_HEADS = (3 * D_MODEL) // (4 * HEAD_DIM)
GQA_KV_HEADS = GQA_HEADS // 3
GQA_GROUP = GQA_HEADS // GQA_KV_HEADS
A_HEADS = GQA_HEADS // 2
A_V_DIM = 2 * HEAD_DIM
MEM_HEADS = 4
MEM_LEN = 256
MIX_W = GQA_HEADS * HEAD_DIM
MEM_W = MEM_HEADS * HEAD_DIM
ATTN_W = MIX_W + MEM_W
KV_W = GQA_KV_HEADS * HEAD_DIM
A_QK_W = A_HEADS * 2 * HEAD_DIM
A_IN = 2 * A_QK_W + MIX_W + MEM_W
GQA_IN = MIX_W + 2 * KV_W + MEM_W
A_SPLITS = (A_QK_W, 2 * A_QK_W, 2 * A_QK_W + MIX_W)
GQA_SPLITS = (MIX_W, MIX_W + KV_W, MIX_W + 2 * KV_W)
WINDOW = 128
BLOCK = 128
GRID_W = 64
ROPE_THETA = 10000.0
N_EXPERTS = 16
EC_FACTOR = 2
D_EXPERT = ((8 * D_MODEL // 3 + 127) // 128) * 128
ALPHA = (2 * DEPTH) ** 0.25
BETA = (8 * DEPTH) ** -0.25
LN_EPS = 1e-5
RMS_EPS = 1e-6
N_A = (DEPTH + 2) // 3
N_B = (DEPTH + 1) // 3
N_C = DEPTH // 3

kernel_name = "hybrid_diff_window_axial_ec_encoder"

F32 = jnp.float32


def layer_norm(x, g, b):
    xf = x.astype(F32)
    mu = jnp.mean(xf, -1, keepdims=True)
    var = jnp.mean(jnp.square(xf - mu), -1, keepdims=True)
    return ((xf - mu) * lax.rsqrt(var + LN_EPS) * g.astype(F32) + b.astype(F32)).astype(x.dtype)


def rms_norm(x, g):
    xf = x.astype(F32)
    return (xf * lax.rsqrt(jnp.mean(xf * xf, -1, keepdims=True) + RMS_EPS) * g.astype(F32)).astype(x.dtype)


def rope_tables(pos, dim):
    inv = ROPE_THETA ** (-jnp.arange(0, dim, 2, dtype=F32) / dim)
    ang = pos.astype(F32)[:, None] * inv[None, :]
    return jnp.cos(ang), jnp.sin(ang)


def apply_rope(x, cos, sin):
    d2 = x.shape[-1] // 2
    x1 = x[..., :d2].astype(F32)
    x2 = x[..., d2:].astype(F32)
    c = cos[:, None, :]
    s = sin[:, None, :]
    return jnp.concatenate([x1 * c - x2 * s, x2 * c + x1 * s], -1).astype(x.dtype)


def apply_axial_rope(x, cos_r, sin_r, cos_c, sin_c):
    half = x.shape[-1] // 2
    return jnp.concatenate([apply_rope(x[..., :half], cos_r, sin_r),
                            apply_rope(x[..., half:], cos_c, sin_c)], -1)


def diff_mixer(x, w_in, lam_p, subln_g, lam_init, cos1, sin1):
    B, S, _ = x.shape
    q, k, v, qm = jnp.split(x @ w_in, A_SPLITS, axis=-1)
    q = apply_rope(q.reshape(B, S, 2 * A_HEADS, HEAD_DIM), cos1, sin1).reshape(B, S, A_HEADS, 2, HEAD_DIM)
    k = apply_rope(k.reshape(B, S, 2 * A_HEADS, HEAD_DIM), cos1, sin1).reshape(B, S, A_HEADS, 2, HEAD_DIM)
    v = v.reshape(B, S, A_HEADS, A_V_DIM)
    lp = lam_p.astype(F32)
    lam = jnp.exp(jnp.sum(lp[0] * lp[1])) - jnp.exp(jnp.sum(lp[2] * lp[3])) + lam_init
    scale = HEAD_DIM ** -0.5
    nb = S // BLOCK
    qb = q.reshape(B, nb, BLOCK, A_HEADS, 2, HEAD_DIM).transpose(1, 0, 2, 3, 4, 5)

    def one_block(qblk):
        s = jnp.einsum('bqhmd,bshmd->bhmqs', qblk, k).astype(F32) * scale
        p = jax.nn.softmax(s, axis=-1)
        w = (p[:, :, 0] - lam * p[:, :, 1]).astype(v.dtype)
        return jnp.einsum('bhqs,bshe->bqhe', w, v)

    o = lax.map(one_block, qb).transpose(1, 0, 2, 3, 4).reshape(B, S, A_HEADS, A_V_DIM)
    o = rms_norm(o, subln_g) * (1.0 - lam_init)
    return o.reshape(B, S, MIX_W), qm.reshape(B, S, MEM_HEADS, HEAD_DIM)


def window_sink_mixer(x, w_in, sink, cos1, sin1):
    B, S, _ = x.shape
    q, k, v, qm = jnp.split(x @ w_in, GQA_SPLITS, axis=-1)
    q = apply_rope(q.reshape(B, S, GQA_HEADS, HEAD_DIM), cos1, sin1)
    k = apply_rope(k.reshape(B, S, GQA_KV_HEADS, HEAD_DIM), cos1, sin1)
    v = v.reshape(B, S, GQA_KV_HEADS, HEAD_DIM)
    nb = S // BLOCK
    pad = ((0, 0), (BLOCK, BLOCK), (0, 0), (0, 0))
    kp = jnp.pad(k, pad).reshape(B, nb + 2, BLOCK, GQA_KV_HEADS, HEAD_DIM)
    vp = jnp.pad(v, pad).reshape(B, nb + 2, BLOCK, GQA_KV_HEADS, HEAD_DIM)
    kw = jnp.concatenate([kp[:, :-2], kp[:, 1:-1], kp[:, 2:]], axis=2)
    vw = jnp.concatenate([vp[:, :-2], vp[:, 1:-1], vp[:, 2:]], axis=2)
    qb = q.reshape(B, nb, BLOCK, GQA_KV_HEADS, GQA_GROUP, HEAD_DIM)
    s = jnp.einsum('bnqkgd,bnjkd->bnkgqj', qb, kw).astype(F32) * (HEAD_DIM ** -0.5)
    qpos = jnp.arange(nb)[:, None] * BLOCK + jnp.arange(BLOCK)[None, :]
    kpos = jnp.arange(nb)[:, None] * BLOCK - BLOCK + jnp.arange(3 * BLOCK)[None, :]
    rel = kpos[:, None, :] - qpos[:, :, None]
    valid = (jnp.abs(rel) <= WINDOW) & (kpos[:, None, :] >= 0) & (kpos[:, None, :] < S)
    s = jnp.where(valid[None, :, None, None, :, :], s, -jnp.inf)
    sk = sink.astype(F32).reshape(1, 1, GQA_KV_HEADS, GQA_GROUP, 1, 1)
    m = jnp.maximum(jnp.max(s, -1, keepdims=True), sk)
    e = jnp.exp(s - m)
    p = (e / (jnp.sum(e, -1, keepdims=True) + jnp.exp(sk - m))).astype(v.dtype)
    o = jnp.einsum('bnkgqj,bnjkd->bnqkgd', p, vw).reshape(B, S, MIX_W)
    return o, qm.reshape(B, S, MEM_HEADS, HEAD_DIM)


def axial_mixer(x, w_in, qk_g, cos_r, sin_r, cos_c, sin_c):
    B, S, _ = x.shape
    q, k, v, qm = jnp.split(x @ w_in, GQA_SPLITS, axis=-1)
    q = rms_norm(q.reshape(B, S, GQA_HEADS, HEAD_DIM), qk_g[0])
    k = rms_norm(k.reshape(B, S, GQA_KV_HEADS, HEAD_DIM), qk_g[1])
    q = apply_axial_rope(q, cos_r, sin_r, cos_c, sin_c)
    k = apply_axial_rope(k, cos_r, sin_r, cos_c, sin_c)
    v = v.reshape(B, S, GQA_KV_HEADS, HEAD_DIM)
    nb = S // BLOCK
    qb = q.reshape(B, nb, BLOCK, GQA_KV_HEADS, GQA_GROUP, HEAD_DIM).transpose(1, 0, 2, 3, 4, 5)
    scale = HEAD_DIM ** -0.5

    def one_block(qblk):
        s = jnp.einsum('bqkgd,bskd->bkgqs', qblk, k).astype(F32) * scale
        p = jax.nn.softmax(s, axis=-1).astype(v.dtype)
        return jnp.einsum('bkgqs,bskd->bqkgd', p, v)

    o = lax.map(one_block, qb).transpose(1, 0, 2, 3, 4, 5).reshape(B, S, MIX_W)
    return o, qm.reshape(B, S, MEM_HEADS, HEAD_DIM)


def memory_attend(qm, mem, w_kv):
    B, M, _ = mem.shape
    km, vm = jnp.split(mem @ w_kv, 2, axis=-1)
    km = km.reshape(B, M, MEM_HEADS, HEAD_DIM)
    vm = vm.reshape(B, M, MEM_HEADS, HEAD_DIM)
    s = jnp.einsum('bshd,bmhd->bhsm', qm, km).astype(F32) * (HEAD_DIM ** -0.5)
    p = jax.nn.softmax(s, axis=-1).astype(vm.dtype)
    o = jnp.einsum('bhsm,bmhd->bshd', p, vm)
    return o.reshape(qm.shape[0], qm.shape[1], MEM_W)


def expert_choice_ffn(x, w_router, w_gate_up, w_down):
    B, S, D = x.shape
    n_tok = B * S
    cap = EC_FACTOR * n_tok // N_EXPERTS
    xf = x.reshape(n_tok, D)
    aff = jax.nn.softmax((xf @ w_router).astype(F32), axis=-1)
    gates, idx = lax.top_k(aff.T, cap)
    xin = xf[idx]
    g, u = jnp.split(jnp.einsum('ecd,edf->ecf', xin, w_gate_up), 2, axis=-1)
    h = jax.nn.silu(g) * u
    y = jnp.einsum('ecf,efd->ecd', h, w_down) * gates[..., None].astype(x.dtype)
    out = jnp.zeros_like(xf).at[idx.reshape(-1)].add(y.reshape(-1, D))
    return out.reshape(B, S, D)


def encode(x, mem, a_w_in, a_lambda, a_subln, b_w_in, b_sink, c_w_in, c_qk_norm,
           w_mem_kv, w_o, ln_mix, w_router, w_gate_up, w_down, ln_ffn):
    B, S, _ = x.shape
    ROWS = S // GRID_W
    cos1, sin1 = rope_tables(jnp.arange(S), HEAD_DIM)
    rows = jnp.repeat(jnp.arange(ROWS), GRID_W)
    cols = jnp.tile(jnp.arange(GRID_W), ROWS)
    cos_r, sin_r = rope_tables(rows, HEAD_DIM // 2)
    cos_c, sin_c = rope_tables(cols, HEAD_DIM // 2)
    for i in range(DEPTH):
        kind, j = i % N_MIXERS, i // N_MIXERS
        if kind == 0:
            lam_init = 0.8 - 0.6 * math.exp(-0.3 * i)
            mix, qm = diff_mixer(x, a_w_in[j], a_lambda[j], a_subln[j], lam_init, cos1, sin1)
        elif kind == 1:
            mix, qm = window_sink_mixer(x, b_w_in[j], b_sink[j], cos1, sin1)
        else:
            mix, qm = axial_mixer(x, c_w_in[j], c_qk_norm[j], cos_r, sin_r, cos_c, sin_c)
        mem_out = memory_attend(qm, mem, w_mem_kv[i])
        sub = jnp.concatenate([mix, mem_out], axis=-1) @ w_o[i]
        x = layer_norm(ALPHA * x + sub, ln_mix[i, 0], ln_mix[i, 1])
        f = expert_choice_ffn(x, w_router[i], w_gate_up[i], w_down[i])
        x = layer_norm(ALPHA * x + f, ln_ffn[i, 0], ln_ffn[i, 1])
    return x


def setup_inputs(seed: int = 0) -> dict:
    key = jax.random.key(seed)
    ks = jax.random.split(key, 20)
    nrm = jax.random.normal
    D = D_MODEL

    def ln_params(k):
        k1, k2 = jax.random.split(k)
        return jnp.stack([1.0 + 0.02 * nrm(k1, (DEPTH, D), F32), 0.02 * nrm(k2, (DEPTH, D), F32)], axis=1)

    return {
        "x_prompt": nrm(ks[0], (BATCH, SEQ, D), F32),
        "x_sample": nrm(ks[1], (DEC_BATCH, DEC_SEQ, D), F32),
        "mem_prompt": nrm(ks[2], (BATCH, MEM_LEN, D), F32),
        "mem_sample": nrm(ks[3], (DEC_BATCH, MEM_LEN, D), F32),
        "a_w_in": nrm(ks[4], (N_A, D, A_IN), F32) * D ** -0.5,
        "a_lambda": 0.1 * nrm(ks[5], (N_A, 4, HEAD_DIM), F32),
        "a_subln": 1.0 + 0.02 * nrm(ks[6], (N_A, A_V_DIM), F32),
        "b_w_in": nrm(ks[7], (N_B, D, GQA_IN), F32) * D ** -0.5,
        "b_sink": 0.5 * nrm(ks[8], (N_B, GQA_HEADS), F32),
        "c_w_in": nrm(ks[9], (N_C, D, GQA_IN), F32) * D ** -0.5,
        "c_qk_norm": 1.0 + 0.02 * nrm(ks[10], (N_C, 2, HEAD_DIM), F32),
        "w_mem_kv": nrm(ks[11], (DEPTH, D, 2 * MEM_W), F32) * D ** -0.5,
        "w_o": nrm(ks[12], (DEPTH, ATTN_W, D), F32) * (ATTN_W ** -0.5 * BETA),
        "ln_mix": ln_params(ks[13]),
        "w_router": nrm(ks[14], (DEPTH, D, N_EXPERTS), F32) * D ** -0.5,
        "w_gate_up": nrm(ks[15], (DEPTH, N_EXPERTS, D, 2 * D_EXPERT), F32) * D ** -0.5,
        "w_down": nrm(ks[16], (DEPTH, N_EXPERTS, D_EXPERT, D), F32) * (D_EXPERT ** -0.5 * BETA),
        "ln_ffn": ln_params(ks[17]),
    }


def reference(x_prompt, x_sample, mem_prompt, mem_sample, a_w_in, a_lambda, a_subln,
              b_w_in, b_sink, c_w_in, c_qk_norm, w_mem_kv, w_o, ln_mix, w_router,
              w_gate_up, w_down, ln_ffn):
    y_prompt = encode(x_prompt, mem_prompt, a_w_in, a_lambda, a_subln, b_w_in, b_sink,
                      c_w_in, c_qk_norm, w_mem_kv, w_o, ln_mix, w_router, w_gate_up,
                      w_down, ln_ffn)
    y_sample = encode(x_sample, mem_sample, a_w_in, a_lambda, a_subln, b_w_in, b_sink,
                      c_w_in, c_qk_norm, w_mem_kv, w_o, ln_mix, w_router, w_gate_up,
                      w_down, ln_ffn)
    return (y_prompt, y_sample)
```

```python
import functools
import math

import jax
import jax.numpy as jnp
from jax import lax
from jax.experimental import pallas as pl
from jax.experimental.pallas import tpu as pltpu

F32 = jnp.float32
BF16 = jnp.bfloat16

HEAD_DIM = 64
GQA_KV_HEADS = 4
GQA_GROUP = 3
A_HEADS = 6
A_V_DIM = 2 * HEAD_DIM
MEM_HEADS = 4
MIX_W = 768
MEM_W = MEM_HEADS * HEAD_DIM
KV_W = GQA_KV_HEADS * HEAD_DIM
N_MIXERS = 3
WINDOW = 128
GRID_W = 64
ROPE_THETA = 10000.0
N_EXPERTS = 16
EC_FACTOR = 2
LN_EPS = 1e-5
RMS_EPS = 1e-6
QK_SCALE = HEAD_DIM ** -0.5
LANES = 128
VMEM_LIMIT = 52 * 1024 * 1024

_NT = (((1,), (1,)), ((), ()))


def _cparams(sem):
    return pltpu.CompilerParams(dimension_semantics=sem, vmem_limit_bytes=VMEM_LIMIT)


def _layer_norm(h, g, b):
    mu = jnp.mean(h, axis=-1, keepdims=True)
    d = h - mu
    var = jnp.mean(d * d, axis=-1, keepdims=True)
    return d * lax.rsqrt(var + LN_EPS) * g + b


def _inproj_kernel(x_ref, w_ref, cos_ref, sin_ref, g_ref, bd_ref,
                   q_ref, k_ref, v_ref, qm_ref, *, qw, kw, vw, half, qk_norm):
    xb = x_ref[...].astype(BF16)
    tm = xb.shape[0]
    lane = lax.broadcasted_iota(jnp.int32, (tm, LANES), 1)
    first = (lane % (2 * half)) < half
    cos = cos_ref[...]
    sin = sin_ref[...]

    def rope(ch):
        rot = jnp.where(first, pltpu.roll(ch, LANES - half, 1), pltpu.roll(ch, half, 1))
        return ch * cos + rot * sin

    def head_norm(ch, g):
        sq = ch * ch
        hi = sq.astype(BF16)
        lo = (sq - hi.astype(F32)).astype(BF16)
        ss = (jnp.dot(hi, bd_ref[...], preferred_element_type=F32)
              + jnp.dot(lo, bd_ref[...], preferred_element_type=F32))
        return ch * lax.rsqrt(ss * (1.0 / HEAD_DIM) + RMS_EPS) * g

    yq = jnp.dot(xb, w_ref[:, 0:qw], preferred_element_type=F32)
    for c in range(qw // LANES):
        ch = yq[:, c * LANES:(c + 1) * LANES]
        if qk_norm:
            ch = head_norm(ch, g_ref[0:1, :])
        q_ref[:, c * LANES:(c + 1) * LANES] = (rope(ch) * QK_SCALE).astype(BF16)
    yk = jnp.dot(xb, w_ref[:, qw:qw + kw], preferred_element_type=F32)
    for c in range(kw // LANES):
        ch = yk[:, c * LANES:(c + 1) * LANES]
        if qk_norm:
            ch = head_norm(ch, g_ref[1:2, :])
        k_ref[:, c * LANES:(c + 1) * LANES] = rope(ch).astype(BF16)
    v_ref[...] = jnp.dot(xb, w_ref[:, qw + kw:qw + kw + vw],
                         preferred_element_type=F32).astype(BF16)
    qm = jnp.dot(xb, w_ref[:, qw + kw + vw:qw + kw + vw + MEM_W], preferred_element_type=F32)
    qm_ref[...] = (qm * QK_SCALE).astype(BF16)


def _inproj(x2d, w, cos, sin, g2, bd, *, seq, qw, kw, vw, half, qk_norm):
    n, d = x2d.shape
    tm = min(512, seq)
    nb = seq // tm
    kern = functools.partial(_inproj_kernel, qw=qw, kw=kw, vw=vw, half=half, qk_norm=qk_norm)
    return pl.pallas_call(
        kern,
        grid=(n // tm,),
        in_specs=[
            pl.BlockSpec((tm, d), lambda i: (i, 0)),
            pl.BlockSpec(w.shape, lambda i: (0, 0)),
            pl.BlockSpec((tm, LANES), lambda i: (i % nb, 0)),
            pl.BlockSpec((tm, LANES), lambda i: (i % nb, 0)),
            pl.BlockSpec((2, LANES), lambda i: (0, 0)),
            pl.BlockSpec((LANES, LANES), lambda i: (0, 0)),
        ],
        out_specs=[
            pl.BlockSpec((tm, qw), lambda i: (i, 0)),
            pl.BlockSpec((tm, kw), lambda i: (i, 0)),
            pl.BlockSpec((tm, vw), lambda i: (i, 0)),
            pl.BlockSpec((tm, MEM_W), lambda i: (i, 0)),
        ],
        out_shape=[
            jax.ShapeDtypeStruct((n, qw), BF16),
            jax.ShapeDtypeStruct((n, kw), BF16),
            jax.ShapeDtypeStruct((n, vw), BF16),
            jax.ShapeDtypeStruct((n, MEM_W), BF16),
        ],
        compiler_params=_cparams(("parallel",)),
        name="inproj",
    )(x2d, w, cos, sin, g2, bd)


def _matmul_kernel(x_ref, w_ref, o_ref):
    o_ref[...] = jnp.dot(x_ref[...].astype(BF16), w_ref[...],
                         preferred_element_type=F32).astype(o_ref.dtype)


def _matmul(x2d, w):
    n, d = x2d.shape
    tm = min(512, n)
    return pl.pallas_call(
        _matmul_kernel,
        grid=(n // tm,),
        in_specs=[pl.BlockSpec((tm, d), lambda i: (i, 0)),
                  pl.BlockSpec(w.shape, lambda i: (0, 0))],
        out_specs=pl.BlockSpec((tm, w.shape[1]), lambda i: (i, 0)),
        out_shape=jax.ShapeDtypeStruct((n, w.shape[1]), BF16),
        compiler_params=_cparams(("parallel",)),
        name="matmul",
    )(x2d, w)


def _flash_update(q, kk, vv, m, l, acc):
    s = lax.dot_general(q, kk, _NT, preferred_element_type=F32)
    m_new = jnp.maximum(m, jnp.max(s, axis=-1, keepdims=True))
    a = jnp.exp(m - m_new)
    p = jnp.exp(s - m_new)
    l = a * l + jnp.sum(p, axis=-1, keepdims=True)
    acc = a * acc + jnp.dot(p.astype(BF16), vv, preferred_element_type=F32)
    return m_new, l, acc


def _flash_init(rows, dv):
    return (jnp.full((rows, 1), -jnp.inf, F32), jnp.zeros((rows, 1), F32),
            jnp.zeros((rows, dv), F32))


def _gqa_kernel(q_ref, k_ref, v_ref, o_ref, *, group, tq, tk, nk):
    q = q_ref[0, 0].reshape(group * tq, HEAD_DIM)

    def body(j, carry):
        off = pl.multiple_of(j * tk, tk)
        kk = k_ref[0, 0, pl.ds(off, tk), :]
        vv = v_ref[0, 0, pl.ds(off, tk), :]
        return _flash_update(q, kk, vv, *carry)

    m, l, acc = lax.fori_loop(0, nk, body, _flash_init(group * tq, HEAD_DIM))
    o_ref[0, 0] = (acc / l).reshape(group, tq, HEAD_DIM).astype(o_ref.dtype)


def _gqa_attention(q, k, v):
    b, kvh, group, s, _ = q.shape
    sk = k.shape[2]
    tq = min(256, s)
    tk = min(512, sk)
    kern = functools.partial(_gqa_kernel, group=group, tq=tq, tk=tk, nk=sk // tk)
    return pl.pallas_call(
        kern,
        grid=(b, kvh, s // tq),
        in_specs=[
            pl.BlockSpec((1, 1, group, tq, HEAD_DIM), lambda bi, h, i: (bi, h, 0, i, 0)),
            pl.BlockSpec((1, 1, sk, HEAD_DIM), lambda bi, h, i: (bi, h, 0, 0)),
            pl.BlockSpec((1, 1, sk, HEAD_DIM), lambda bi, h, i: (bi, h, 0, 0)),
        ],
        out_specs=pl.BlockSpec((1, 1, group, tq, HEAD_DIM), lambda bi, h, i: (bi, h, 0, i, 0)),
        out_shape=jax.ShapeDtypeStruct(q.shape, BF16),
        compiler_params=_cparams(("parallel", "parallel", "parallel")),
        name="gqa_attention",
    )(q, k, v)


def _diff_kernel(lam_ref, g_ref, q_ref, k_ref, v_ref, o_ref, *, tq, tk, nk, lam_init):
    lp = lam_ref[...]
    lam = (jnp.exp(jnp.sum(lp[0:1] * lp[1:2], axis=-1, keepdims=True))
           - jnp.exp(jnp.sum(lp[2:3] * lp[3:4], axis=-1, keepdims=True)) + lam_init)
    q0 = q_ref[0, 0, 0]
    q1 = q_ref[0, 0, 1]

    def body(j, carry):
        off = pl.multiple_of(j * tk, tk)
        vv = v_ref[0, 0, pl.ds(off, tk), :]
        c0 = _flash_update(q0, k_ref[0, 0, 0, pl.ds(off, tk), :], vv, *carry[0:3])
        c1 = _flash_update(q1, k_ref[0, 0, 1, pl.ds(off, tk), :], vv, *carry[3:6])
        return c0 + c1

    init = _flash_init(tq, A_V_DIM) + _flash_init(tq, A_V_DIM)
    _, l0, a0, _, l1, a1 = lax.fori_loop(0, nk, body, init)
    o = a0 / l0 - lam * (a1 / l1)
    ms = jnp.mean(o * o, axis=-1, keepdims=True)
    o = o * lax.rsqrt(ms + RMS_EPS) * g_ref[...] * (1.0 - lam_init)
    o_ref[0] = o.astype(o_ref.dtype)


def _diff_attention(q, k, v, lam_p, subln_g, lam_init):
    b, h, _, s, _ = q.shape
    tq = min(256, s)
    tk = min(512, s)
    kern = functools.partial(_diff_kernel, tq=tq, tk=tk, nk=s // tk, lam_init=lam_init)
    return pl.pallas_call(
        kern,
        grid=(b, h, s // tq),
        in_specs=[
            pl.BlockSpec((4, HEAD_DIM), lambda bi, hi, i: (0, 0)),
            pl.BlockSpec((1, A_V_DIM), lambda bi, hi, i: (0, 0)),
            pl.BlockSpec((1, 1, 2, tq, HEAD_DIM), lambda bi, hi, i: (bi, hi, 0, i, 0)),
            pl.BlockSpec((1, 1, 2, s, HEAD_DIM), lambda bi, hi, i: (bi, hi, 0, 0, 0)),
            pl.BlockSpec((1, 1, s, A_V_DIM), lambda bi, hi, i: (bi, hi, 0, 0)),
        ],
        out_specs=pl.BlockSpec((1, tq, A_V_DIM), lambda bi, hi, i: (bi, i, hi)),
        out_shape=jax.ShapeDtypeStruct((b, s, h * A_V_DIM), BF16),
        compiler_params=_cparams(("parallel", "parallel", "parallel")),
        name="diff_attention",
    )(lam_p, subln_g, q, k, v)


def _window_kernel(sink_ref, q_ref, k_ref, v_ref, o_ref, *, group, tq, win, seq):
    h = pl.program_id(1)
    q0 = pl.program_id(2) * tq
    start = pl.multiple_of(jnp.clip(q0 - WINDOW, 0, seq - win), LANES)
    kw = k_ref[0, 0, pl.ds(start, win), :]
    vw = v_ref[0, 0, pl.ds(start, win), :]
    qpos = q0 + lax.broadcasted_iota(jnp.int32, (tq, win), 0)
    kpos = start + lax.broadcasted_iota(jnp.int32, (tq, win), 1)
    valid = jnp.abs(kpos - qpos) <= WINDOW
    for g in range(group):
        sk = sink_ref[h * group + g]
        s = lax.dot_general(q_ref[0, 0, g], kw, _NT, preferred_element_type=F32)
        s = jnp.where(valid, s, -jnp.inf)
        m = jnp.maximum(jnp.max(s, axis=-1, keepdims=True), sk)
        e = jnp.exp(s - m)
        den = jnp.sum(e, axis=-1, keepdims=True) + jnp.exp(sk - m)
        o = jnp.dot(e.astype(BF16), vw, preferred_element_type=F32) / den
        o_ref[0, 0, g] = o.astype(o_ref.dtype)


def _window_attention(q, k, v, sink):
    b, kvh, group, s, _ = q.shape
    tq = min(512, s)
    win = min(s, tq + 2 * WINDOW)
    kern = functools.partial(_window_kernel, group=group, tq=tq, win=win, seq=s)
    return pl.pallas_call(
        kern,
        grid=(b, kvh, s // tq),
        in_specs=[
            pl.BlockSpec(memory_space=pltpu.SMEM),
            pl.BlockSpec((1, 1, group, tq, HEAD_DIM), lambda bi, h, i: (bi, h, 0, i, 0)),
            pl.BlockSpec((1, 1, s, HEAD_DIM), lambda bi, h, i: (bi, h, 0, 0)),
            pl.BlockSpec((1, 1, s, HEAD_DIM), lambda bi, h, i: (bi, h, 0, 0)),
        ],
        out_specs=pl.BlockSpec((1, 1, group, tq, HEAD_DIM), lambda bi, h, i: (bi, h, 0, i, 0)),
        out_shape=jax.ShapeDtypeStruct(q.shape, BF16),
        compiler_params=_cparams(("parallel", "parallel", "parallel")),
        name="window_attention",
    )(sink, q, k, v)


def _oproj_kernel(mix_ref, mem_ref, x_ref, wo_ref, ln_ref, wr_ref,
                  xo_ref, xb_ref, aff_ref, *, alpha):
    sub = (jnp.dot(mix_ref[...], wo_ref[0:MIX_W, :], preferred_element_type=F32)
           + jnp.dot(mem_ref[...], wo_ref[MIX_W:MIX_W + MEM_W, :], preferred_element_type=F32))
    y = _layer_norm(alpha * x_ref[...] + sub, ln_ref[0:1, :], ln_ref[1:2, :])
    xo_ref[...] = y
    yb = y.astype(BF16)
    xb_ref[...] = yb
    logits = jnp.dot(yb, wr_ref[...], preferred_element_type=F32)
    e = jnp.exp(logits - jnp.max(logits, axis=-1, keepdims=True))
    aff_ref[...] = e / jnp.sum(e, axis=-1, keepdims=True)


def _oproj(mix, mem, x2d, wo, ln, wr, alpha):
    n, d = x2d.shape
    tm = min(512, n)
    kern = functools.partial(_oproj_kernel, alpha=alpha)
    return pl.pallas_call(
        kern,
        grid=(n // tm,),
        in_specs=[
            pl.BlockSpec((tm, MIX_W), lambda i: (i, 0)),
            pl.BlockSpec((tm, MEM_W), lambda i: (i, 0)),
            pl.BlockSpec((tm, d), lambda i: (i, 0)),
            pl.BlockSpec(wo.shape, lambda i: (0, 0)),
            pl.BlockSpec((2, d), lambda i: (0, 0)),
            pl.BlockSpec(wr.shape, lambda i: (0, 0)),
        ],
        out_specs=[
            pl.BlockSpec((tm, d), lambda i: (i, 0)),
            pl.BlockSpec((tm, d), lambda i: (i, 0)),
            pl.BlockSpec((tm, N_EXPERTS), lambda i: (i, 0)),
        ],
        out_shape=[
            jax.ShapeDtypeStruct((n, d), F32),
            jax.ShapeDtypeStruct((n, d), BF16),
            jax.ShapeDtypeStruct((n, N_EXPERTS), F32),
        ],
        compiler_params=_cparams(("parallel",)),
        name="oproj_ln_router",
    )(mix, mem, x2d, wo, ln, wr)


def _ffn_kernel(x_ref, wg_ref, wu_ref, wd_ref, gate_ref, y_ref, acc_ref, *, nf):
    f = pl.program_id(2)
    x = x_ref[0]
    g = jnp.dot(x, wg_ref[0], preferred_element_type=F32)
    u = jnp.dot(x, wu_ref[0], preferred_element_type=F32)
    h = (g / (1.0 + jnp.exp(-g))) * u
    contrib = jnp.dot(h.astype(BF16), wd_ref[0], preferred_element_type=F32)

    @pl.when(f == 0)
    def _():
        acc_ref[...] = contrib

    @pl.when(f > 0)
    def _():
        acc_ref[...] += contrib

    @pl.when(f == nf - 1)
    def _():
        y_ref[0] = acc_ref[...] * gate_ref[0]


def _expert_ffn(xin, w_gu, w_dn, gates):
    e, c, d = xin.shape
    fdim = w_dn.shape[1]
    tc = min(1024, c)
    tf = 256
    nf = fdim // tf
    kern = functools.partial(_ffn_kernel, nf=nf)
    return pl.pallas_call(
        kern,
        grid=(e, c // tc, nf),
        in_specs=[
            pl.BlockSpec((1, tc, d), lambda ei, ci, fi: (ei, ci, 0)),
            pl.BlockSpec((1, d, tf), lambda ei, ci, fi: (ei, 0, fi)),
            pl.BlockSpec((1, d, tf), lambda ei, ci, fi: (ei, 0, fi + nf)),
            pl.BlockSpec((1, tf, d), lambda ei, ci, fi: (ei, fi, 0)),
            pl.BlockSpec((1, tc, 1), lambda ei, ci, fi: (ei, ci, 0)),
        ],
        out_specs=pl.BlockSpec((1, tc, d), lambda ei, ci, fi: (ei, ci, 0)),
        out_shape=jax.ShapeDtypeStruct((e, c, d), F32),
        scratch_shapes=[pltpu.VMEM((tc, d), F32)],
        compiler_params=_cparams(("parallel", "parallel", "arbitrary")),
        name="expert_ffn",
    )(xin, w_gu, w_gu, w_dn, gates)


def _ln2_kernel(x_ref, f_ref, ln_ref, o_ref, *, alpha):
    o_ref[...] = _layer_norm(alpha * x_ref[...] + f_ref[...], ln_ref[0:1, :], ln_ref[1:2, :])


def _residual_ln(x2d, f2d, ln, alpha):
    n, d = x2d.shape
    tm = min(1024, n)
    return pl.pallas_call(
        functools.partial(_ln2_kernel, alpha=alpha),
        grid=(n // tm,),
        in_specs=[pl.BlockSpec((tm, d), lambda i: (i, 0)),
                  pl.BlockSpec((tm, d), lambda i: (i, 0)),
                  pl.BlockSpec((2, d), lambda i: (0, 0))],
        out_specs=pl.BlockSpec((tm, d), lambda i: (i, 0)),
        out_shape=jax.ShapeDtypeStruct((n, d), F32),
        compiler_params=_cparams(("parallel",)),
        name="residual_ln",
    )(x2d, f2d, ln)


def _rope_tables(seq):
    def tab(pos, dim):
        inv = ROPE_THETA ** (-jnp.arange(0, dim, 2, dtype=F32) / dim)
        ang = pos.astype(F32)[:, None] * inv[None, :]
        return jnp.cos(ang), jnp.sin(ang)

    c1, s1 = tab(jnp.arange(seq), HEAD_DIM)
    cos1 = jnp.tile(c1, (1, LANES // (HEAD_DIM // 2)))
    sin1 = jnp.tile(jnp.concatenate([-s1, s1], axis=-1), (1, LANES // HEAD_DIM))
    rows = jnp.arange(seq) // GRID_W
    cols = jnp.arange(seq) % GRID_W
    cr, sr = tab(rows, HEAD_DIM // 2)
    cc, sc = tab(cols, HEAD_DIM // 2)
    cos2 = jnp.tile(jnp.concatenate([cr, cr, cc, cc], axis=-1), (1, LANES // HEAD_DIM))
    sin2 = jnp.tile(jnp.concatenate([-sr, sr, -sc, sc], axis=-1), (1, LANES // HEAD_DIM))
    return (cos1, sin1), (cos2, sin2)


def _encode(x, mem, p):
    b, s, d = x.shape
    n = b * s
    m_len = mem.shape[1]
    depth = p["w_o"].shape[0]
    alpha = (2 * depth) ** 0.25
    tabs1, tabs2 = _rope_tables(s)
    bd = (jnp.arange(LANES)[:, None] // HEAD_DIM == jnp.arange(LANES)[None, :] // HEAD_DIM).astype(BF16)
    ones2 = jnp.ones((2, LANES), F32)
    cap = EC_FACTOR * n // N_EXPERTS
    x2d = x.reshape(n, d)
    mem2d = mem.reshape(b * m_len, d)

    for i in range(depth):
        kind, j = i % N_MIXERS, i // N_MIXERS
        if kind == 0:
            q, k, v, qm = _inproj(x2d, p["a_w_in"][j], *tabs1, ones2, bd, seq=s,
                                  qw=MIX_W, kw=MIX_W, vw=MIX_W, half=HEAD_DIM // 2, qk_norm=False)
            lam_init = 0.8 - 0.6 * math.exp(-0.3 * i)
            qh = q.reshape(b, s, A_HEADS, 2, HEAD_DIM).transpose(0, 2, 3, 1, 4)
            kh = k.reshape(b, s, A_HEADS, 2, HEAD_DIM).transpose(0, 2, 3, 1, 4)
            vh = v.reshape(b, s, A_HEADS, A_V_DIM).transpose(0, 2, 1, 3)
            mix = _diff_attention(qh, kh, vh, p["a_lambda"][j], p["a_subln"][j][None, :], lam_init)
            mix = mix.reshape(n, MIX_W)
        else:
            if kind == 1:
                q, k, v, qm = _inproj(x2d, p["b_w_in"][j], *tabs1, ones2, bd, seq=s,
                                      qw=MIX_W, kw=KV_W, vw=KV_W, half=HEAD_DIM // 2, qk_norm=False)
            else:
                g2 = jnp.tile(p["c_qk_norm"][j], (1, LANES // HEAD_DIM))
                q, k, v, qm = _inproj(x2d, p["c_w_in"][j], *tabs2, g2, bd, seq=s,
                                      qw=MIX_W, kw=KV_W, vw=KV_W, half=HEAD_DIM // 4, qk_norm=True)
            qh = q.reshape(b, s, GQA_KV_HEADS, GQA_GROUP, HEAD_DIM).transpose(0, 2, 3, 1, 4)
            kh = k.reshape(b, s, GQA_KV_HEADS, HEAD_DIM).transpose(0, 2, 1, 3)
            vh = v.reshape(b, s, GQA_KV_HEADS, HEAD_DIM).transpose(0, 2, 1, 3)
            if kind == 1:
                oh = _window_attention(qh, kh, vh, p["b_sink"][j])
            else:
                oh = _gqa_attention(qh, kh, vh)
            mix = oh.transpose(0, 3, 1, 2, 4).reshape(n, MIX_W)

        mkv = _matmul(mem2d, p["w_mem_kv"][i]).reshape(b, m_len, 2, MEM_HEADS, HEAD_DIM)
        km = mkv[:, :, 0].transpose(0, 2, 1, 3)
        vm = mkv[:, :, 1].transpose(0, 2, 1, 3)
        qmh = qm.reshape(b, s, MEM_HEADS, 1, HEAD_DIM).transpose(0, 2, 3, 1, 4)
        mo = _gqa_attention(qmh, km, vm).transpose(0, 3, 1, 2, 4).reshape(n, MEM_W)

        x2d, xb, aff = _oproj(mix, mo, x2d, p["w_o"][i], p["ln_mix"][i], p["w_router"][i], alpha)

        gates, idx = lax.top_k(aff.T, cap)
        xin = xb[idx]
        y = _expert_ffn(xin, p["w_gate_up"][i], p["w_down"][i], gates[..., None])
        f = jnp.zeros((n, d), F32).at[idx.reshape(-1)].add(y.reshape(-1, d))
        x2d = _residual_ln(x2d, f, p["ln_ffn"][i], alpha)
    return x2d.reshape(b, s, d)


def kernel(x_prompt, x_sample, mem_prompt, mem_sample, a_w_in, a_lambda, a_subln, b_w_in, b_sink, c_w_in, c_qk_norm, w_mem_kv, w_o, ln_mix, w_router, w_gate_up, w_down, ln_ffn):
    p = dict(
        a_w_in=a_w_in.astype(BF16), a_lambda=a_lambda, a_subln=a_subln,
        b_w_in=b_w_in.astype(BF16), b_sink=b_sink,
        c_w_in=c_w_in.astype(BF16), c_qk_norm=c_qk_norm,
        w_mem_kv=w_mem_kv.astype(BF16), w_o=w_o.astype(BF16), ln_mix=ln_mix,
        w_router=w_router.astype(BF16), w_gate_up=w_gate_up.astype(BF16),
        w_down=w_down.astype(BF16), ln_ffn=ln_ffn,
    )
    return (_encode(x_prompt, mem_prompt, p), _encode(x_sample, mem_sample, p))
```

```python
import functools
import math

import jax
import jax.numpy as jnp
from jax import lax
from jax.experimental import pallas as pl
from jax.experimental.pallas import tpu as pltpu

F32 = jnp.float32
BF16 = jnp.bfloat16

HEAD_DIM = 64
GQA_KV_HEADS = 4
GQA_GROUP = 3
A_HEADS = 6
A_V_DIM = 2 * HEAD_DIM
MEM_HEADS = 4
MIX_W = 768
MEM_W = MEM_HEADS * HEAD_DIM
KV_W = GQA_KV_HEADS * HEAD_DIM
N_MIXERS = 3
WINDOW = 128
GRID_W = 64
ROPE_THETA = 10000.0
N_EXPERTS = 16
EC_FACTOR = 2
LN_EPS = 1e-5
RMS_EPS = 1e-6
LOG2E = 1.4426950408889634
QK_SCALE = HEAD_DIM ** -0.5 * LOG2E
LANES = 128
ONES_ROWS = 16
VMEM_LIMIT = 52 * 1024 * 1024

_NT = (((1,), (1,)), ((), ()))
_TN = (((0,), (0,)), ((), ()))


def _cparams(sem):
    return pltpu.CompilerParams(dimension_semantics=sem, vmem_limit_bytes=VMEM_LIMIT)


def _layer_norm(h, g, b):
    mu = jnp.mean(h, axis=-1, keepdims=True)
    d = h - mu
    var = jnp.mean(d * d, axis=-1, keepdims=True)
    return d * lax.rsqrt(var + LN_EPS) * g + b


def _lane_tile(a, width):
    return jnp.concatenate([a] * (width // a.shape[1]), axis=1)


def _inproj_kernel(x_ref, wqT_ref, wvT_ref, wmT_ref, wk_ref, cosT_ref, sinT_ref, cos_ref, sin_ref,
                   gT_ref, g_ref, bd_ref, qT_ref, k_ref, vT_ref, qmT_ref, *, half, qk_norm, vchunk):
    xb = x_ref[...].astype(BF16)
    tm = xb.shape[0]
    cosT = cosT_ref[...]
    sinT = sinT_ref[...]
    nparts = HEAD_DIM // half

    yqT = lax.dot_general(wqT_ref[...], xb, _NT, preferred_element_type=F32)
    for h in range(yqT.shape[0] // HEAD_DIM):
        xh = yqT[h * HEAD_DIM:(h + 1) * HEAD_DIM]
        if qk_norm:
            ms = jnp.mean(xh * xh, axis=0, keepdims=True)
            xh = xh * lax.rsqrt(ms + RMS_EPS) * _lane_tile(gT_ref[...], tm)
        rot = jnp.concatenate([xh[(p ^ 1) * half:((p ^ 1) + 1) * half] for p in range(nparts)], axis=0)
        qT_ref[0, h * HEAD_DIM:(h + 1) * HEAD_DIM, :] = ((xh * cosT + rot * sinT) * QK_SCALE).astype(BF16)

    yvT = lax.dot_general(wvT_ref[...], xb, _NT, preferred_element_type=F32).astype(BF16)
    for c in range(tm // vchunk):
        vT_ref[0, c] = yvT[:, c * vchunk:(c + 1) * vchunk]
    ymT = lax.dot_general(wmT_ref[...], xb, _NT, preferred_element_type=F32)
    qmT_ref[0] = (ymT * QK_SCALE).astype(BF16)

    lane = lax.broadcasted_iota(jnp.int32, (tm, LANES), 1)
    first = (lane % (2 * half)) < half
    cos = cos_ref[...]
    sin = sin_ref[...]
    yk = jnp.dot(xb, wk_ref[...], preferred_element_type=F32)
    for c in range(yk.shape[1] // LANES):
        ch = yk[:, c * LANES:(c + 1) * LANES]
        if qk_norm:
            sq = ch * ch
            hi = sq.astype(BF16)
            lo = (sq - hi.astype(F32)).astype(BF16)
            ss = (jnp.dot(hi, bd_ref[...], preferred_element_type=F32)
                  + jnp.dot(lo, bd_ref[...], preferred_element_type=F32))
            ch = ch * lax.rsqrt(ss * (1.0 / HEAD_DIM) + RMS_EPS) * g_ref[...]
        rot = jnp.where(first, pltpu.roll(ch, LANES - half, 1), pltpu.roll(ch, half, 1))
        ch = (ch * cos + rot * sin).astype(BF16)
        k_ref[0, 2 * c] = ch[:, 0:HEAD_DIM]
        k_ref[0, 2 * c + 1] = ch[:, HEAD_DIM:LANES]


def _inproj(x2d, wqT, wvT, wmT, wk, tabsT, tabs, gT, g, bd, *, batch, seq, half, qk_norm, vchunk):
    n, d = x2d.shape
    tm = min(512, seq)
    vchunk = min(vchunk, tm)
    nb = seq // tm
    qw, vw, kw = wqT.shape[0], wvT.shape[0], wk.shape[1]
    kern = functools.partial(_inproj_kernel, half=half, qk_norm=qk_norm, vchunk=vchunk)
    const = lambda b, i: (0, 0)
    return pl.pallas_call(
        kern,
        grid=(batch, nb),
        in_specs=[
            pl.BlockSpec((tm, d), lambda b, i: (b * nb + i, 0)),
            pl.BlockSpec(wqT.shape, const),
            pl.BlockSpec(wvT.shape, const),
            pl.BlockSpec(wmT.shape, const),
            pl.BlockSpec(wk.shape, const),
            pl.BlockSpec((HEAD_DIM, tm), lambda b, i: (0, i)),
            pl.BlockSpec((HEAD_DIM, tm), lambda b, i: (0, i)),
            pl.BlockSpec((tm, LANES), lambda b, i: (i, 0)),
            pl.BlockSpec((tm, LANES), lambda b, i: (i, 0)),
            pl.BlockSpec((HEAD_DIM, LANES), const),
            pl.BlockSpec((1, LANES), const),
            pl.BlockSpec((LANES, LANES), const),
        ],
        out_specs=[
            pl.BlockSpec((1, qw, tm), lambda b, i: (b, 0, i)),
            pl.BlockSpec((1, kw // HEAD_DIM, tm, HEAD_DIM), lambda b, i: (b, 0, i, 0)),
            pl.BlockSpec((1, tm // vchunk, vw, vchunk), lambda b, i: (b, i, 0, 0)),
            pl.BlockSpec((1, MEM_W, tm), lambda b, i: (b, 0, i)),
        ],
        out_shape=[
            jax.ShapeDtypeStruct((batch, qw, seq), BF16),
            jax.ShapeDtypeStruct((batch, kw // HEAD_DIM, seq, HEAD_DIM), BF16),
            jax.ShapeDtypeStruct((batch, seq // vchunk, vw, vchunk), BF16),
            jax.ShapeDtypeStruct((batch, MEM_W, seq), BF16),
        ],
        compiler_params=_cparams(("parallel", "parallel")),
        name="inproj",
    )(x2d, wqT, wvT, wmT, wk, *tabsT, *tabs, gT, g, bd)


def _memproj_kernel(x_ref, wk_ref, wvT_ref, k_ref, vT_ref):
    xb = x_ref[...].astype(BF16)
    yk = jnp.dot(xb, wk_ref[...], preferred_element_type=F32).astype(BF16)
    for h in range(MEM_HEADS):
        k_ref[0, h] = yk[:, h * HEAD_DIM:(h + 1) * HEAD_DIM]
    vT_ref[0, 0] = lax.dot_general(wvT_ref[...], xb, _NT, preferred_element_type=F32).astype(BF16)


def _memproj(mem2d, wk, wvT, batch, m_len):
    d = mem2d.shape[1]
    return pl.pallas_call(
        _memproj_kernel,
        grid=(batch,),
        in_specs=[pl.BlockSpec((m_len, d), lambda b: (b, 0)),
                  pl.BlockSpec(wk.shape, lambda b: (0, 0)),
                  pl.BlockSpec(wvT.shape, lambda b: (0, 0))],
        out_specs=[pl.BlockSpec((1, MEM_HEADS, m_len, HEAD_DIM), lambda b: (b, 0, 0, 0)),
                   pl.BlockSpec((1, 1, MEM_W, m_len), lambda b: (b, 0, 0, 0))],
        out_shape=[jax.ShapeDtypeStruct((batch, MEM_HEADS, m_len, HEAD_DIM), BF16),
                   jax.ShapeDtypeStruct((batch, 1, MEM_W, m_len), BF16)],
        compiler_params=_cparams(("parallel",)),
        name="memproj",
    )(mem2d, wk, wvT)


def _ones_rows(tk):
    return (lax.broadcasted_iota(jnp.int32, (ONES_ROWS, tk), 0) == 0).astype(BF16)


def _flash_step(sT, va, m, acc):
    m_new = jnp.maximum(m, jnp.max(sT, axis=0, keepdims=True))
    a = jnp.exp2(m - m_new)
    pT = jnp.exp2(sT - m_new).astype(BF16)
    return m_new, a * acc + jnp.dot(va, pT, preferred_element_type=F32)


def _gqa_kernel(qT_ref, k_ref, vT_ref, o_ref, *, group, nk, unroll):
    tk = k_ref.shape[3]
    qT = jnp.concatenate([qT_ref[0, g * HEAD_DIM:(g + 1) * HEAD_DIM, :] for g in range(group)], axis=1)
    nq = qT.shape[1]
    tq = nq // group
    ones = _ones_rows(tk)

    def body(j, carry):
        sT = jnp.dot(k_ref[0, 0, j], qT, preferred_element_type=F32)
        va = jnp.concatenate([vT_ref[0, j], ones], axis=0)
        return _flash_step(sT, va, *carry)

    init = (jnp.full((1, nq), -jnp.inf, F32), jnp.zeros((HEAD_DIM + ONES_ROWS, nq), F32))
    _, acc = lax.fori_loop(0, nk, body, init, unroll=unroll)
    o = acc[0:HEAD_DIM] / acc[HEAD_DIM:HEAD_DIM + 1]
    for g in range(group):
        o_ref[0, g * HEAD_DIM:(g + 1) * HEAD_DIM, :] = o[:, g * tq:(g + 1) * tq].astype(o_ref.dtype)


def _gqa_attention(qT, k5, vT5, group, tq):
    b, qrows, s = qT.shape
    _, kvh, nk, tk, _ = k5.shape
    tq = min(tq, s)
    kern = functools.partial(_gqa_kernel, group=group, nk=nk, unroll=2 if nk % 2 == 0 else 1)
    return pl.pallas_call(
        kern,
        grid=(b, kvh, s // tq),
        in_specs=[
            pl.BlockSpec((1, group * HEAD_DIM, tq), lambda bi, h, i: (bi, h, i)),
            pl.BlockSpec((1, 1, nk, tk, HEAD_DIM), lambda bi, h, i: (bi, h, 0, 0, 0)),
            pl.BlockSpec((1, nk, HEAD_DIM, tk), lambda bi, h, i: (bi, 0, h, 0)),
        ],
        out_specs=pl.BlockSpec((1, group * HEAD_DIM, tq), lambda bi, h, i: (bi, h, i)),
        out_shape=jax.ShapeDtypeStruct(qT.shape, BF16),
        compiler_params=_cparams(("parallel", "parallel", "parallel")),
        name="gqa_attention",
    )(qT, k5, vT5)


def _diff_kernel(lam_ref, g_ref, qT_ref, k_ref, vT_ref, o_ref, *, nk, unroll, lam_init):
    tk = k_ref.shape[3]
    tq = qT_ref.shape[2]
    lp = lam_ref[...]
    lam = (jnp.exp(jnp.sum(lp[0:1] * lp[1:2], axis=-1, keepdims=True))
           - jnp.exp(jnp.sum(lp[2:3] * lp[3:4], axis=-1, keepdims=True)) + lam_init)
    q0T = qT_ref[0, 0:HEAD_DIM, :]
    q1T = qT_ref[0, HEAD_DIM:2 * HEAD_DIM, :]
    ones = _ones_rows(tk)

    def body(j, carry):
        sT = jnp.concatenate([jnp.dot(k_ref[0, 0, j], q0T, preferred_element_type=F32),
                              jnp.dot(k_ref[0, 1, j], q1T, preferred_element_type=F32)], axis=1)
        va = jnp.concatenate([vT_ref[0, j], ones], axis=0)
        return _flash_step(sT, va, *carry)

    init = (jnp.full((1, 2 * tq), -jnp.inf, F32), jnp.zeros((A_V_DIM + ONES_ROWS, 2 * tq), F32))
    _, acc = lax.fori_loop(0, nk, body, init, unroll=unroll)
    o = acc[0:A_V_DIM] / acc[A_V_DIM:A_V_DIM + 1]
    o = o[:, 0:tq] - lam * o[:, tq:2 * tq]
    ms = jnp.mean(o * o, axis=0, keepdims=True)
    o = o * lax.rsqrt(ms + RMS_EPS) * _lane_tile(g_ref[...], tq) * (1.0 - lam_init)
    o_ref[0] = o.astype(o_ref.dtype)


def _diff_attention(qT, k5, vT5, lam_p, g_col, lam_init, tq):
    b, qrows, s = qT.shape
    _, _, nk, tk, _ = k5.shape
    tq = min(tq, s)
    kern = functools.partial(_diff_kernel, nk=nk, unroll=2 if nk % 2 == 0 else 1, lam_init=lam_init)
    return pl.pallas_call(
        kern,
        grid=(b, A_HEADS, s // tq),
        in_specs=[
            pl.BlockSpec((4, HEAD_DIM), lambda bi, h, i: (0, 0)),
            pl.BlockSpec((A_V_DIM, LANES), lambda bi, h, i: (0, 0)),
            pl.BlockSpec((1, 2 * HEAD_DIM, tq), lambda bi, h, i: (bi, h, i)),
            pl.BlockSpec((1, 2, nk, tk, HEAD_DIM), lambda bi, h, i: (bi, h, 0, 0, 0)),
            pl.BlockSpec((1, nk, A_V_DIM, tk), lambda bi, h, i: (bi, 0, h, 0)),
        ],
        out_specs=pl.BlockSpec((1, A_V_DIM, tq), lambda bi, h, i: (bi, h, i)),
        out_shape=jax.ShapeDtypeStruct(qT.shape, BF16),
        compiler_params=_cparams(("parallel", "parallel", "parallel")),
        name="diff_attention",
    )(lam_p, g_col, qT, k5, vT5)


def _window_kernel(sink_ref, qT_ref, k_ref, vT_ref, o_ref, *, group, nwc, nchunks):
    h = pl.program_id(1)
    tq = qT_ref.shape[2]
    nq = group * tq
    q0 = pl.program_id(2) * tq
    c0 = jnp.clip(q0 // LANES - WINDOW // LANES, 0, nchunks - nwc)
    qT = jnp.concatenate([qT_ref[0, g * HEAD_DIM:(g + 1) * HEAD_DIM, :] for g in range(group)], axis=1)
    qpos = q0 + lax.broadcasted_iota(jnp.int32, (LANES, nq), 1) % tq
    krow = lax.broadcasted_iota(jnp.int32, (LANES, nq), 0)
    sink = jnp.concatenate(
        [jnp.full((1, tq), sink_ref[h * group + g] * LOG2E, F32) for g in range(group)], axis=1)
    ones = _ones_rows(LANES)
    scores = []
    m = sink
    for c in range(nwc):
        sT = jnp.dot(k_ref[0, 0, c0 + c], qT, preferred_element_type=F32)
        kpos = (c0 + c) * LANES + krow
        sT = jnp.where(jnp.abs(kpos - qpos) <= WINDOW, sT, -jnp.inf)
        scores.append(sT)
        m = jnp.maximum(m, jnp.max(sT, axis=0, keepdims=True))
    acc = jnp.zeros((HEAD_DIM + ONES_ROWS, nq), F32)
    for c in range(nwc):
        va = jnp.concatenate([vT_ref[0, c0 + c], ones], axis=0)
        acc = acc + jnp.dot(va, jnp.exp2(scores[c] - m).astype(BF16), preferred_element_type=F32)
    o = acc[0:HEAD_DIM] / (acc[HEAD_DIM:HEAD_DIM + 1] + jnp.exp2(sink - m))
    for g in range(group):
        o_ref[0, g * HEAD_DIM:(g + 1) * HEAD_DIM, :] = o[:, g * tq:(g + 1) * tq].astype(o_ref.dtype)


def _window_attention(qT, k5, vT5, sink, group, tq):
    b, qrows, s = qT.shape
    _, kvh, nchunks, tk, _ = k5.shape
    tq = min(tq, s)
    nwc = min(nchunks, tq // LANES + 2 * (WINDOW // LANES))
    kern = functools.partial(_window_kernel, group=group, nwc=nwc, nchunks=nchunks)
    return pl.pallas_call(
        kern,
        grid=(b, kvh, s // tq),
        in_specs=[
            pl.BlockSpec(memory_space=pltpu.SMEM),
            pl.BlockSpec((1, group * HEAD_DIM, tq), lambda bi, h, i: (bi, h, i)),
            pl.BlockSpec((1, 1, nchunks, tk, HEAD_DIM), lambda bi, h, i: (bi, h, 0, 0, 0)),
            pl.BlockSpec((1, nchunks, HEAD_DIM, tk), lambda bi, h, i: (bi, 0, h, 0)),
        ],
        out_specs=pl.BlockSpec((1, group * HEAD_DIM, tq), lambda bi, h, i: (bi, h, i)),
        out_shape=jax.ShapeDtypeStruct(qT.shape, BF16),
        compiler_params=_cparams(("parallel", "parallel", "parallel")),
        name="window_attention",
    )(sink, qT, k5, vT5)


def _oproj_kernel(mixT_ref, memT_ref, x_ref, wo_ref, ln_ref, wr_ref,
                  xo_ref, xb_ref, aff_ref, *, alpha):
    sub = (lax.dot_general(mixT_ref[0], wo_ref[0:MIX_W, :], _TN, preferred_element_type=F32)
           + lax.dot_general(memT_ref[0], wo_ref[MIX_W:MIX_W + MEM_W, :], _TN,
                             preferred_element_type=F32))
    y = _layer_norm(alpha * x_ref[...] + sub, ln_ref[0:1, :], ln_ref[1:2, :])
    xo_ref[...] = y
    yb = y.astype(BF16)
    xb_ref[...] = yb
    logits = jnp.dot(yb, wr_ref[...], preferred_element_type=F32)
    e = jnp.exp(logits - jnp.max(logits, axis=-1, keepdims=True))
    aff_ref[...] = e / jnp.sum(e, axis=-1, keepdims=True)


def _oproj(mixT, memT, x2d, wo, ln, wr, alpha, batch, seq):
    n, d = x2d.shape
    tm = min(512, seq)
    nb = seq // tm
    kern = functools.partial(_oproj_kernel, alpha=alpha)
    const = lambda b, i: (0, 0)
    row = lambda b, i: (b * nb + i, 0)
    return pl.pallas_call(
        kern,
        grid=(batch, nb),
        in_specs=[
            pl.BlockSpec((1, MIX_W, tm), lambda b, i: (b, 0, i)),
            pl.BlockSpec((1, MEM_W, tm), lambda b, i: (b, 0, i)),
            pl.BlockSpec((tm, d), row),
            pl.BlockSpec(wo.shape, const),
            pl.BlockSpec((2, d), const),
            pl.BlockSpec(wr.shape, const),
        ],
        out_specs=[
            pl.BlockSpec((tm, d), row),
            pl.BlockSpec((tm, d), row),
            pl.BlockSpec((tm, N_EXPERTS), row),
        ],
        out_shape=[
            jax.ShapeDtypeStruct((n, d), F32),
            jax.ShapeDtypeStruct((n, d), BF16),
            jax.ShapeDtypeStruct((n, N_EXPERTS), F32),
        ],
        compiler_params=_cparams(("parallel", "parallel")),
        name="oproj_ln_router",
    )(mixT, memT, x2d, wo, ln, wr)


def _ffn_kernel(x_ref, wg_ref, wu_ref, wd_ref, gate_ref, y_ref, acc_ref, *, nf):
    f = pl.program_id(2)
    x = x_ref[0]
    g = jnp.dot(x, wg_ref[0], preferred_element_type=F32)
    u = jnp.dot(x, wu_ref[0], preferred_element_type=F32)
    h = (g / (1.0 + jnp.exp(-g))) * u
    contrib = jnp.dot(h.astype(BF16), wd_ref[0], preferred_element_type=F32)

    @pl.when(f == 0)
    def _():
        acc_ref[...] = contrib

    @pl.when(f > 0)
    def _():
        acc_ref[...] += contrib

    @pl.when(f == nf - 1)
    def _():
        y_ref[0] = acc_ref[...] * gate_ref[0]


def _expert_ffn(xin, w_gu, w_dn, gates):
    e, c, d = xin.shape
    fdim = w_dn.shape[1]
    tc = min(1024, c)
    tf = 256
    nf = fdim // tf
    kern = functools.partial(_ffn_kernel, nf=nf)
    return pl.pallas_call(
        kern,
        grid=(e, c // tc, nf),
        in_specs=[
            pl.BlockSpec((1, tc, d), lambda ei, ci, fi: (ei, ci, 0)),
            pl.BlockSpec((1, d, tf), lambda ei, ci, fi: (ei, 0, fi)),
            pl.BlockSpec((1, d, tf), lambda ei, ci, fi: (ei, 0, fi + nf)),
            pl.BlockSpec((1, tf, d), lambda ei, ci, fi: (ei, fi, 0)),
            pl.BlockSpec((1, tc, 1), lambda ei, ci, fi: (ei, ci, 0)),
        ],
        out_specs=pl.BlockSpec((1, tc, d), lambda ei, ci, fi: (ei, ci, 0)),
        out_shape=jax.ShapeDtypeStruct((e, c, d), F32),
        scratch_shapes=[pltpu.VMEM((tc, d), F32)],
        compiler_params=_cparams(("parallel", "parallel", "arbitrary")),
        name="expert_ffn",
    )(xin, w_gu, w_gu, w_dn, gates)


def _ln2_kernel(x_ref, f_ref, ln_ref, o_ref, *, alpha):
    o_ref[...] = _layer_norm(alpha * x_ref[...] + f_ref[...], ln_ref[0:1, :], ln_ref[1:2, :])


def _residual_ln(x2d, f2d, ln, alpha):
    n, d = x2d.shape
    tm = min(1024, n)
    return pl.pallas_call(
        functools.partial(_ln2_kernel, alpha=alpha),
        grid=(n // tm,),
        in_specs=[pl.BlockSpec((tm, d), lambda i: (i, 0)),
                  pl.BlockSpec((tm, d), lambda i: (i, 0)),
                  pl.BlockSpec((2, d), lambda i: (0, 0))],
        out_specs=pl.BlockSpec((tm, d), lambda i: (i, 0)),
        out_shape=jax.ShapeDtypeStruct((n, d), F32),
        compiler_params=_cparams(("parallel",)),
        name="residual_ln",
    )(x2d, f2d, ln)


def _rope_tables(seq):
    def tab(pos, dim):
        inv = ROPE_THETA ** (-jnp.arange(0, dim, 2, dtype=F32) / dim)
        ang = pos.astype(F32)[:, None] * inv[None, :]
        return jnp.cos(ang), jnp.sin(ang)

    c1, s1 = tab(jnp.arange(seq), HEAD_DIM)
    cos1 = jnp.concatenate([c1, c1], axis=-1)
    sin1 = jnp.concatenate([-s1, s1], axis=-1)
    cr, sr = tab(jnp.arange(seq) // GRID_W, HEAD_DIM // 2)
    cc, sc = tab(jnp.arange(seq) % GRID_W, HEAD_DIM // 2)
    cos2 = jnp.concatenate([cr, cr, cc, cc], axis=-1)
    sin2 = jnp.concatenate([-sr, sr, -sc, sc], axis=-1)
    rep = LANES // HEAD_DIM
    nat = lambda c, s: (jnp.tile(c, (1, rep)), jnp.tile(s, (1, rep)))
    return ((cos1.T, sin1.T), nat(cos1, sin1)), ((cos2.T, sin2.T), nat(cos2, sin2))


def _prep_params(p):
    bf = lambda w: w.astype(BF16)
    tr = lambda w: jnp.swapaxes(w, -1, -2).astype(BF16)
    a, bw, cw = p["a_w_in"], p["b_w_in"], p["c_w_in"]
    out = dict(p)
    out.update(
        a_wqT=tr(a[:, :, 0:MIX_W]), a_wk=bf(a[:, :, MIX_W:2 * MIX_W]),
        a_wvT=tr(a[:, :, 2 * MIX_W:3 * MIX_W]), a_wmT=tr(a[:, :, 3 * MIX_W:]),
        w_memk=bf(p["w_mem_kv"][:, :, 0:MEM_W]), w_memvT=tr(p["w_mem_kv"][:, :, MEM_W:]),
        w_o=bf(p["w_o"]), w_router=bf(p["w_router"]),
        w_gate_up=bf(p["w_gate_up"]), w_down=bf(p["w_down"]),
    )
    for name, w in (("b", bw), ("c", cw)):
        out[name + "_wqT"] = tr(w[:, :, 0:MIX_W])
        out[name + "_wk"] = bf(w[:, :, MIX_W:MIX_W + KV_W])
        out[name + "_wvT"] = tr(w[:, :, MIX_W + KV_W:MIX_W + 2 * KV_W])
        out[name + "_wmT"] = tr(w[:, :, MIX_W + 2 * KV_W:])
    return out


def _encode(x, mem, p):
    b, s, d = x.shape
    n = b * s
    m_len = mem.shape[1]
    depth = p["w_o"].shape[0]
    alpha = (2 * depth) ** 0.25
    (tabs1T, tabs1), (tabs2T, tabs2) = _rope_tables(s)
    bd = (jnp.arange(LANES)[:, None] // HEAD_DIM == jnp.arange(LANES)[None, :] // HEAD_DIM).astype(BF16)
    ones_g = jnp.ones((1, LANES), F32)
    ones_gT = jnp.ones((HEAD_DIM, LANES), F32)
    cap = EC_FACTOR * n // N_EXPERTS
    x2d = x.reshape(n, d)
    mem2d = mem.reshape(b * m_len, d)
    tq_full, tk_full = 512, 512

    for i in range(depth):
        kind, j = i % N_MIXERS, i // N_MIXERS
        common = dict(batch=b, seq=s)
        if kind == 0:
            qT, k4, vT5, qmT = _inproj(x2d, p["a_wqT"][j], p["a_wvT"][j], p["a_wmT"][j], p["a_wk"][j],
                                       tabs1T, tabs1, ones_gT, ones_g, bd, half=HEAD_DIM // 2,
                                       qk_norm=False, vchunk=tk_full, **common)
            lam_init = 0.8 - 0.6 * math.exp(-0.3 * i)
            tk = min(tk_full, s)
            k5 = k4.reshape(b, 2 * A_HEADS, s // tk, tk, HEAD_DIM)
            g_col = jnp.broadcast_to(p["a_subln"][j][:, None], (A_V_DIM, LANES))
            mixT = _diff_attention(qT, k5, vT5, p["a_lambda"][j], g_col, lam_init, tq_full)
        elif kind == 1:
            qT, k4, vT5, qmT = _inproj(x2d, p["b_wqT"][j], p["b_wvT"][j], p["b_wmT"][j], p["b_wk"][j],
                                       tabs1T, tabs1, ones_gT, ones_g, bd, half=HEAD_DIM // 2,
                                       qk_norm=False, vchunk=LANES, **common)
            k5 = k4.reshape(b, GQA_KV_HEADS, s // LANES, LANES, HEAD_DIM)
            mixT = _window_attention(qT, k5, vT5, p["b_sink"][j], GQA_GROUP, tq_full)
        else:
            gq = jnp.broadcast_to(p["c_qk_norm"][j][0][:, None], (HEAD_DIM, LANES))
            gk = jnp.tile(p["c_qk_norm"][j][1], LANES // HEAD_DIM)[None, :]
            qT, k4, vT5, qmT = _inproj(x2d, p["c_wqT"][j], p["c_wvT"][j], p["c_wmT"][j], p["c_wk"][j],
                                       tabs2T, tabs2, gq, gk, bd, half=HEAD_DIM // 4,
                                       qk_norm=True, vchunk=tk_full, **common)
            tk = min(tk_full, s)
            k5 = k4.reshape(b, GQA_KV_HEADS, s // tk, tk, HEAD_DIM)
            mixT = _gqa_attention(qT, k5, vT5, GQA_GROUP, tq_full)

        km, vmT = _memproj(mem2d, p["w_memk"][i], p["w_memvT"][i], b, m_len)
        memT = _gqa_attention(qmT, km.reshape(b, MEM_HEADS, 1, m_len, HEAD_DIM), vmT, 1, 2 * tq_full)

        x2d, xb, aff = _oproj(mixT, memT, x2d, p["w_o"][i], p["ln_mix"][i], p["w_router"][i], alpha, b, s)

        gates, idx = lax.top_k(aff.T, cap)
        xin = xb[idx]
        y = _expert_ffn(xin, p["w_gate_up"][i], p["w_down"][i], gates[..., None])
        f = jnp.zeros((n, d), F32).at[idx.reshape(-1)].add(y.reshape(-1, d))
        x2d = _residual_ln(x2d, f, p["ln_ffn"][i], alpha)
    return x2d.reshape(b, s, d)


def kernel(x_prompt, x_sample, mem_prompt, mem_sample, a_w_in, a_lambda, a_subln, b_w_in, b_sink, c_w_in, c_qk_norm, w_mem_kv, w_o, ln_mix, w_router, w_gate_up, w_down, ln_ffn):
    p = _prep_params(dict(
        a_w_in=a_w_in, a_lambda=a_lambda, a_subln=a_subln, b_w_in=b_w_in, b_sink=b_sink,
        c_w_in=c_w_in, c_qk_norm=c_qk_norm, w_mem_kv=w_mem_kv, w_o=w_o, ln_mix=ln_mix,
        w_router=w_router, w_gate_up=w_gate_up, w_down=w_down, ln_ffn=ln_ffn))
    return (_encode(x_prompt, mem_prompt, p), _encode(x_sample, mem_sample, p))
```

```python
import functools
import math

import jax
import jax.numpy as jnp
from jax import lax
from jax.experimental import pallas as pl
from jax.experimental.pallas import tpu as pltpu

F32 = jnp.float32
BF16 = jnp.bfloat16

HEAD_DIM = 64
GQA_KV_HEADS = 4
GQA_GROUP = 3
A_HEADS = 6
A_V_DIM = 2 * HEAD_DIM
MEM_HEADS = 4
MIX_W = 768
MEM_W = MEM_HEADS * HEAD_DIM
KV_W = GQA_KV_HEADS * HEAD_DIM
N_MIXERS = 3
WINDOW = 128
GRID_W = 64
ROPE_THETA = 10000.0
N_EXPERTS = 16
EC_FACTOR = 2
LN_EPS = 1e-5
RMS_EPS = 1e-6
LOG2E = 1.4426950408889634
QK_SCALE = HEAD_DIM ** -0.5 * LOG2E
LANES = 128
ONES_ROWS = 16
VMEM_LIMIT = 52 * 1024 * 1024

_NT = (((1,), (1,)), ((), ()))
_TN = (((0,), (0,)), ((), ()))


def _cparams(grid_rank):
    return pltpu.CompilerParams(dimension_semantics=("arbitrary",) * grid_rank,
                                vmem_limit_bytes=VMEM_LIMIT)


def _layer_norm(h, g, b):
    mu = jnp.mean(h, axis=-1, keepdims=True)
    d = h - mu
    var = jnp.mean(d * d, axis=-1, keepdims=True)
    return d * lax.rsqrt(var + LN_EPS) * g + b


def _lane_tile(a, width):
    return jnp.concatenate([a] * (width // a.shape[1]), axis=1)


def _inproj_kernel(x_ref, wqT_ref, wvT_ref, wmT_ref, wk_ref, cosT_ref, sinT_ref, cos_ref, sin_ref,
                   gT_ref, g_ref, bd_ref, qT_ref, k_ref, vT_ref, qmT_ref, *, half, qk_norm, vchunk):
    xb = x_ref[...].astype(BF16)
    tm = xb.shape[0]
    cosT = cosT_ref[...]
    sinT = sinT_ref[...]
    nparts = HEAD_DIM // half

    yqT = lax.dot_general(wqT_ref[...], xb, _NT, preferred_element_type=F32)
    for h in range(yqT.shape[0] // HEAD_DIM):
        xh = yqT[h * HEAD_DIM:(h + 1) * HEAD_DIM]
        if qk_norm:
            ms = jnp.mean(xh * xh, axis=0, keepdims=True)
            xh = xh * lax.rsqrt(ms + RMS_EPS) * _lane_tile(gT_ref[...], tm)
        rot = jnp.concatenate([xh[(p ^ 1) * half:((p ^ 1) + 1) * half] for p in range(nparts)], axis=0)
        qT_ref[0, h * HEAD_DIM:(h + 1) * HEAD_DIM, :] = ((xh * cosT + rot * sinT) * QK_SCALE).astype(BF16)

    yvT = lax.dot_general(wvT_ref[...], xb, _NT, preferred_element_type=F32).astype(BF16)
    for c in range(tm // vchunk):
        vT_ref[0, c] = yvT[:, c * vchunk:(c + 1) * vchunk]
    ymT = lax.dot_general(wmT_ref[...], xb, _NT, preferred_element_type=F32)
    qmT_ref[0] = (ymT * QK_SCALE).astype(BF16)

    lane = lax.broadcasted_iota(jnp.int32, (tm, LANES), 1)
    first = (lane % (2 * half)) < half
    cos = cos_ref[...]
    sin = sin_ref[...]
    yk = jnp.dot(xb, wk_ref[...], preferred_element_type=F32)
    for c in range(yk.shape[1] // LANES):
        ch = yk[:, c * LANES:(c + 1) * LANES]
        if qk_norm:
            sq = ch * ch
            hi = sq.astype(BF16)
            lo = (sq - hi.astype(F32)).astype(BF16)
            ss = (jnp.dot(hi, bd_ref[...], preferred_element_type=F32)
                  + jnp.dot(lo, bd_ref[...], preferred_element_type=F32))
            ch = ch * lax.rsqrt(ss * (1.0 / HEAD_DIM) + RMS_EPS) * g_ref[...]
        rot = jnp.where(first, pltpu.roll(ch, LANES - half, 1), pltpu.roll(ch, half, 1))
        ch = (ch * cos + rot * sin).astype(BF16)
        k_ref[0, 2 * c] = ch[:, 0:HEAD_DIM]
        k_ref[0, 2 * c + 1] = ch[:, HEAD_DIM:LANES]


def _inproj(x2d, wqT, wvT, wmT, wk, tabsT, tabs, gT, g, bd, *, batch, seq, half, qk_norm, vchunk):
    n, d = x2d.shape
    tm = min(512, seq)
    vchunk = min(vchunk, tm)
    nb = seq // tm
    qw, vw, kw = wqT.shape[0], wvT.shape[0], wk.shape[1]
    kern = functools.partial(_inproj_kernel, half=half, qk_norm=qk_norm, vchunk=vchunk)
    const = lambda b, i: (0, 0)
    return pl.pallas_call(
        kern,
        grid=(batch, nb),
        in_specs=[
            pl.BlockSpec((tm, d), lambda b, i: (b * nb + i, 0)),
            pl.BlockSpec(wqT.shape, const),
            pl.BlockSpec(wvT.shape, const),
            pl.BlockSpec(wmT.shape, const),
            pl.BlockSpec(wk.shape, const),
            pl.BlockSpec((HEAD_DIM, tm), lambda b, i: (0, i)),
            pl.BlockSpec((HEAD_DIM, tm), lambda b, i: (0, i)),
            pl.BlockSpec((tm, LANES), lambda b, i: (i, 0)),
            pl.BlockSpec((tm, LANES), lambda b, i: (i, 0)),
            pl.BlockSpec((HEAD_DIM, LANES), const),
            pl.BlockSpec((1, LANES), const),
            pl.BlockSpec((LANES, LANES), const),
        ],
        out_specs=[
            pl.BlockSpec((1, qw, tm), lambda b, i: (b, 0, i)),
            pl.BlockSpec((1, kw // HEAD_DIM, tm, HEAD_DIM), lambda b, i: (b, 0, i, 0)),
            pl.BlockSpec((1, tm // vchunk, vw, vchunk), lambda b, i: (b, i, 0, 0)),
            pl.BlockSpec((1, MEM_W, tm), lambda b, i: (b, 0, i)),
        ],
        out_shape=[
            jax.ShapeDtypeStruct((batch, qw, seq), BF16),
            jax.ShapeDtypeStruct((batch, kw // HEAD_DIM, seq, HEAD_DIM), BF16),
            jax.ShapeDtypeStruct((batch, seq // vchunk, vw, vchunk), BF16),
            jax.ShapeDtypeStruct((batch, MEM_W, seq), BF16),
        ],
        compiler_params=_cparams(2),
        name="inproj",
    )(x2d, wqT, wvT, wmT, wk, *tabsT, *tabs, gT, g, bd)


def _memproj_kernel(x_ref, wk_ref, wvT_ref, k_ref, vT_ref):
    xb = x_ref[...].astype(BF16)
    yk = jnp.dot(xb, wk_ref[...], preferred_element_type=F32).astype(BF16)
    for h in range(MEM_HEADS):
        k_ref[0, h] = yk[:, h * HEAD_DIM:(h + 1) * HEAD_DIM]
    vT_ref[0, 0] = lax.dot_general(wvT_ref[...], xb, _NT, preferred_element_type=F32).astype(BF16)


def _memproj(mem2d, wk, wvT, batch, m_len):
    d = mem2d.shape[1]
    return pl.pallas_call(
        _memproj_kernel,
        grid=(batch,),
        in_specs=[pl.BlockSpec((m_len, d), lambda b: (b, 0)),
                  pl.BlockSpec(wk.shape, lambda b: (0, 0)),
                  pl.BlockSpec(wvT.shape, lambda b: (0, 0))],
        out_specs=[pl.BlockSpec((1, MEM_HEADS, m_len, HEAD_DIM), lambda b: (b, 0, 0, 0)),
                   pl.BlockSpec((1, 1, MEM_W, m_len), lambda b: (b, 0, 0, 0))],
        out_shape=[jax.ShapeDtypeStruct((batch, MEM_HEADS, m_len, HEAD_DIM), BF16),
                   jax.ShapeDtypeStruct((batch, 1, MEM_W, m_len), BF16)],
        compiler_params=_cparams(1),
        name="memproj",
    )(mem2d, wk, wvT)


def _ones_rows(tk):
    return (lax.broadcasted_iota(jnp.int32, (ONES_ROWS, tk), 0) == 0).astype(BF16)


def _flash_step(sT, va, m, acc):
    m_new = jnp.maximum(m, jnp.max(sT, axis=0, keepdims=True))
    a = jnp.exp2(m - m_new)
    pT = jnp.exp2(sT - m_new).astype(BF16)
    return m_new, a * acc + jnp.dot(va, pT, preferred_element_type=F32)


def _gqa_kernel(qT_ref, k_ref, vT_ref, o_ref, *, group, nk, unroll):
    tk = k_ref.shape[3]
    qT = jnp.concatenate([qT_ref[0, g * HEAD_DIM:(g + 1) * HEAD_DIM, :] for g in range(group)], axis=1)
    nq = qT.shape[1]
    tq = nq // group
    ones = _ones_rows(tk)

    def body(j, carry):
        sT = jnp.dot(k_ref[0, 0, j], qT, preferred_element_type=F32)
        va = jnp.concatenate([vT_ref[0, j], ones], axis=0)
        return _flash_step(sT, va, *carry)

    init = (jnp.full((1, nq), -jnp.inf, F32), jnp.zeros((HEAD_DIM + ONES_ROWS, nq), F32))
    _, acc = lax.fori_loop(0, nk, body, init, unroll=unroll)
    o = acc[0:HEAD_DIM] / acc[HEAD_DIM:HEAD_DIM + 1]
    for g in range(group):
        o_ref[0, g * HEAD_DIM:(g + 1) * HEAD_DIM, :] = o[:, g * tq:(g + 1) * tq].astype(o_ref.dtype)


def _gqa_attention(qT, k5, vT5, group, tq):
    b, qrows, s = qT.shape
    _, kvh, nk, tk, _ = k5.shape
    tq = min(tq, s)
    kern = functools.partial(_gqa_kernel, group=group, nk=nk, unroll=2 if nk % 2 == 0 else 1)
    return pl.pallas_call(
        kern,
        grid=(b, kvh, s // tq),
        in_specs=[
            pl.BlockSpec((1, group * HEAD_DIM, tq), lambda bi, h, i: (bi, h, i)),
            pl.BlockSpec((1, 1, nk, tk, HEAD_DIM), lambda bi, h, i: (bi, h, 0, 0, 0)),
            pl.BlockSpec((1, nk, HEAD_DIM, tk), lambda bi, h, i: (bi, 0, h, 0)),
        ],
        out_specs=pl.BlockSpec((1, group * HEAD_DIM, tq), lambda bi, h, i: (bi, h, i)),
        out_shape=jax.ShapeDtypeStruct(qT.shape, BF16),
        compiler_params=_cparams(3),
        name="gqa_attention",
    )(qT, k5, vT5)


def _diff_kernel(lam_ref, g_ref, qT_ref, k_ref, vT_ref, o_ref, *, nk, unroll, lam_init):
    tk = k_ref.shape[3]
    tq = qT_ref.shape[2]
    lp = lam_ref[...]
    lam = (jnp.exp(jnp.sum(lp[0:1] * lp[1:2], axis=-1, keepdims=True))
           - jnp.exp(jnp.sum(lp[2:3] * lp[3:4], axis=-1, keepdims=True)) + lam_init)
    q0T = qT_ref[0, 0:HEAD_DIM, :]
    q1T = qT_ref[0, HEAD_DIM:2 * HEAD_DIM, :]
    ones = _ones_rows(tk)

    def body(j, carry):
        sT = jnp.concatenate([jnp.dot(k_ref[0, 0, j], q0T, preferred_element_type=F32),
                              jnp.dot(k_ref[0, 1, j], q1T, preferred_element_type=F32)], axis=1)
        va = jnp.concatenate([vT_ref[0, j], ones], axis=0)
        return _flash_step(sT, va, *carry)

    init = (jnp.full((1, 2 * tq), -jnp.inf, F32), jnp.zeros((A_V_DIM + ONES_ROWS, 2 * tq), F32))
    _, acc = lax.fori_loop(0, nk, body, init, unroll=unroll)
    o = acc[0:A_V_DIM] / acc[A_V_DIM:A_V_DIM + 1]
    o = o[:, 0:tq] - lam * o[:, tq:2 * tq]
    ms = jnp.mean(o * o, axis=0, keepdims=True)
    o = o * lax.rsqrt(ms + RMS_EPS) * _lane_tile(g_ref[...], tq) * (1.0 - lam_init)
    o_ref[0] = o.astype(o_ref.dtype)


def _diff_attention(qT, k5, vT5, lam_p, g_col, lam_init, tq):
    b, qrows, s = qT.shape
    _, _, nk, tk, _ = k5.shape
    tq = min(tq, s)
    kern = functools.partial(_diff_kernel, nk=nk, unroll=2 if nk % 2 == 0 else 1, lam_init=lam_init)
    return pl.pallas_call(
        kern,
        grid=(b, A_HEADS, s // tq),
        in_specs=[
            pl.BlockSpec((4, HEAD_DIM), lambda bi, h, i: (0, 0)),
            pl.BlockSpec((A_V_DIM, LANES), lambda bi, h, i: (0, 0)),
            pl.BlockSpec((1, 2 * HEAD_DIM, tq), lambda bi, h, i: (bi, h, i)),
            pl.BlockSpec((1, 2, nk, tk, HEAD_DIM), lambda bi, h, i: (bi, h, 0, 0, 0)),
            pl.BlockSpec((1, nk, A_V_DIM, tk), lambda bi, h, i: (bi, 0, h, 0)),
        ],
        out_specs=pl.BlockSpec((1, A_V_DIM, tq), lambda bi, h, i: (bi, h, i)),
        out_shape=jax.ShapeDtypeStruct(qT.shape, BF16),
        compiler_params=_cparams(3),
        name="diff_attention",
    )(lam_p, g_col, qT, k5, vT5)


def _window_kernel(sink_ref, qT_ref, k_ref, vT_ref, o_ref, *, group, nwc, nchunks):
    h = pl.program_id(1)
    tq = qT_ref.shape[2]
    nq = group * tq
    q0 = pl.program_id(2) * tq
    c0 = jnp.clip(q0 // LANES - WINDOW // LANES, 0, nchunks - nwc)
    qT = jnp.concatenate([qT_ref[0, g * HEAD_DIM:(g + 1) * HEAD_DIM, :] for g in range(group)], axis=1)
    qpos = q0 + lax.broadcasted_iota(jnp.int32, (LANES, nq), 1) % tq
    krow = lax.broadcasted_iota(jnp.int32, (LANES, nq), 0)
    sink = jnp.concatenate(
        [jnp.full((1, tq), sink_ref[h * group + g] * LOG2E, F32) for g in range(group)], axis=1)
    ones = _ones_rows(LANES)
    scores = []
    m = sink
    for c in range(nwc):
        sT = jnp.dot(k_ref[0, 0, c0 + c], qT, preferred_element_type=F32)
        kpos = (c0 + c) * LANES + krow
        sT = jnp.where(jnp.abs(kpos - qpos) <= WINDOW, sT, -jnp.inf)
        scores.append(sT)
        m = jnp.maximum(m, jnp.max(sT, axis=0, keepdims=True))
    acc = jnp.zeros((HEAD_DIM + ONES_ROWS, nq), F32)
    for c in range(nwc):
        va = jnp.concatenate([vT_ref[0, c0 + c], ones], axis=0)
        acc = acc + jnp.dot(va, jnp.exp2(scores[c] - m).astype(BF16), preferred_element_type=F32)
    o = acc[0:HEAD_DIM] / (acc[HEAD_DIM:HEAD_DIM + 1] + jnp.exp2(sink - m))
    for g in range(group):
        o_ref[0, g * HEAD_DIM:(g + 1) * HEAD_DIM, :] = o[:, g * tq:(g + 1) * tq].astype(o_ref.dtype)


def _window_attention(qT, k5, vT5, sink, group, tq):
    b, qrows, s = qT.shape
    _, kvh, nchunks, tk, _ = k5.shape
    tq = min(tq, s)
    nwc = min(nchunks, tq // LANES + 2 * (WINDOW // LANES))
    kern = functools.partial(_window_kernel, group=group, nwc=nwc, nchunks=nchunks)
    return pl.pallas_call(
        kern,
        grid=(b, kvh, s // tq),
        in_specs=[
            pl.BlockSpec(memory_space=pltpu.SMEM),
            pl.BlockSpec((1, group * HEAD_DIM, tq), lambda bi, h, i: (bi, h, i)),
            pl.BlockSpec((1, 1, nchunks, tk, HEAD_DIM), lambda bi, h, i: (bi, h, 0, 0, 0)),
            pl.BlockSpec((1, nchunks, HEAD_DIM, tk), lambda bi, h, i: (bi, 0, h, 0)),
        ],
        out_specs=pl.BlockSpec((1, group * HEAD_DIM, tq), lambda bi, h, i: (bi, h, i)),
        out_shape=jax.ShapeDtypeStruct(qT.shape, BF16),
        compiler_params=_cparams(3),
        name="window_attention",
    )(sink, qT, k5, vT5)


def _oproj_kernel(mixT_ref, memT_ref, x_ref, wo_ref, ln_ref, wrT_ref, xo_ref, affT_ref, *, alpha):
    sub = (lax.dot_general(mixT_ref[0], wo_ref[0:MIX_W, :], _TN, preferred_element_type=F32)
           + lax.dot_general(memT_ref[0], wo_ref[MIX_W:MIX_W + MEM_W, :], _TN,
                             preferred_element_type=F32))
    y = _layer_norm(alpha * x_ref[...] + sub, ln_ref[0:1, :], ln_ref[1:2, :])
    xo_ref[...] = y
    logitsT = lax.dot_general(wrT_ref[...], y.astype(BF16), _NT, preferred_element_type=F32)
    e = jnp.exp(logitsT - jnp.max(logitsT, axis=0, keepdims=True))
    affT_ref[...] = e / jnp.sum(e, axis=0, keepdims=True)


def _oproj(mixT, memT, x2d, wo, ln, wrT, alpha, batch, seq):
    n, d = x2d.shape
    tm = min(512, seq)
    nb = seq // tm
    kern = functools.partial(_oproj_kernel, alpha=alpha)
    const = lambda b, i: (0, 0)
    row = lambda b, i: (b * nb + i, 0)
    return pl.pallas_call(
        kern,
        grid=(batch, nb),
        in_specs=[
            pl.BlockSpec((1, MIX_W, tm), lambda b, i: (b, 0, i)),
            pl.BlockSpec((1, MEM_W, tm), lambda b, i: (b, 0, i)),
            pl.BlockSpec((tm, d), row),
            pl.BlockSpec(wo.shape, const),
            pl.BlockSpec((2, d), const),
            pl.BlockSpec(wrT.shape, const),
        ],
        out_specs=[
            pl.BlockSpec((tm, d), row),
            pl.BlockSpec((N_EXPERTS, tm), lambda b, i: (0, b * nb + i)),
        ],
        out_shape=[
            jax.ShapeDtypeStruct((n, d), F32),
            jax.ShapeDtypeStruct((N_EXPERTS, n), F32),
        ],
        compiler_params=_cparams(2),
        name="oproj_ln_router",
    )(mixT, memT, x2d, wo, ln, wrT)


def _route_select_kernel(affT_ref, tri_ref, ones_ref, low_ref, posm_ref, off_ref, *, cap):
    a = affT_ref[0]
    bits = lax.bitcast_convert_type(a, jnp.int32)

    def count(mask):
        return jnp.sum(jnp.sum(mask.astype(F32), axis=1, keepdims=True), axis=0, keepdims=True)

    def bisect(i, t):
        cand = t | jnp.left_shift(jnp.int32(1), 30 - i)
        return jnp.where(count(bits >= cand) >= cap, cand, t)

    t = lax.fori_loop(0, 31, bisect, jnp.zeros((1, 1), jnp.int32))

    def excl_cumsum(mask):
        mb = mask.astype(BF16)
        incl = jnp.dot(mb, tri_ref[...], preferred_element_type=F32)
        tot = jnp.dot(mb, ones_ref[...], preferred_element_type=F32)
        before = jnp.dot(low_ref[...], tot.astype(BF16), preferred_element_type=F32)
        return before + incl - mask.astype(F32), before

    gt = bits > t
    eq = bits == t
    need = cap - count(gt)
    rank, _ = excl_cumsum(eq)
    sel = gt | (eq & (rank < need))
    pos, before = excl_cumsum(sel)
    posm_ref[0] = jnp.where(sel, pos, -1.0).astype(jnp.int32)
    off_ref[0] = before[:, 0:1].astype(jnp.int32)


def _route_select(affT3, cap):
    e, r, _ = affT3.shape
    tri = (jnp.arange(LANES)[:, None] <= jnp.arange(LANES)[None, :]).astype(BF16)
    ones = jnp.ones((LANES, LANES), BF16)
    low = (jnp.arange(r)[None, :] < jnp.arange(r)[:, None]).astype(BF16)
    posm, off = pl.pallas_call(
        functools.partial(_route_select_kernel, cap=cap),
        grid=(e,),
        in_specs=[pl.BlockSpec((1, r, LANES), lambda ei: (ei, 0, 0)),
                  pl.BlockSpec((LANES, LANES), lambda ei: (0, 0)),
                  pl.BlockSpec((LANES, LANES), lambda ei: (0, 0)),
                  pl.BlockSpec((r, r), lambda ei: (0, 0))],
        out_specs=[pl.BlockSpec((1, r, LANES), lambda ei: (ei, 0, 0)),
                   pl.BlockSpec((1, r, 1), lambda ei: (ei, 0, 0))],
        out_shape=[jax.ShapeDtypeStruct((e, r, LANES), jnp.int32),
                   jax.ShapeDtypeStruct((e, r, 1), jnp.int32)],
        compiler_params=_cparams(1),
        name="route_select",
    )(affT3, tri, ones, low)
    return posm, off.reshape(e, r)


COMPACT_ROWS = 16


def _route_compact_kernel(off_ref, posm_ref, affT_ref, idx_ref, gate_ref, acc_ref, *, nrows, ncb):
    e = pl.program_id(0)
    acc_ref[...] = jnp.zeros_like(acc_ref)
    slot = lax.broadcasted_iota(jnp.int32, (LANES, LANES), 0)
    sub = lax.broadcasted_iota(jnp.int32, (COMPACT_ROWS, LANES), 0)
    lane = lax.broadcasted_iota(jnp.int32, (COMPACT_ROWS, LANES), 1).astype(F32)

    def body(r, carry):
        cb = lax.shift_right_logical(off_ref[e, r], 7)
        rel = posm_ref[0, pl.ds(r, 1), :] - cb * LANES
        a = affT_ref[0, pl.ds(r, 1), :]
        hi = a.astype(BF16).astype(F32)
        mid = (a - hi).astype(BF16).astype(F32)
        lo = a - hi - mid
        vals = jnp.where(sub == 0, lane, jnp.where(sub == 1, jnp.asarray(r, F32), jnp.where(
            sub == 2, hi, jnp.where(sub == 3, mid, jnp.where(sub == 4, lo, 0.0))))).astype(BF16)
        for part in range(2):
            hit = ((rel - part * LANES) == slot).astype(BF16)
            acc_ref[cb + part] += lax.dot_general(vals, hit, _NT, preferred_element_type=F32)
        return carry

    lax.fori_loop(0, nrows, body, 0, unroll=8)
    acc = acc_ref[0:ncb]
    idx_ref[0] = (acc[:, 1:2, :] * LANES + acc[:, 0:1, :]).astype(jnp.int32)
    gate_ref[0] = acc[:, 2:3, :] + acc[:, 3:4, :] + acc[:, 4:5, :]


def _route_compact(off, posm, affT3, cap):
    e, r, _ = posm.shape
    ncb = cap // LANES
    idx, gate = pl.pallas_call(
        functools.partial(_route_compact_kernel, nrows=r, ncb=ncb),
        grid=(e,),
        in_specs=[pl.BlockSpec(memory_space=pltpu.SMEM),
                  pl.BlockSpec((1, r, LANES), lambda ei: (ei, 0, 0)),
                  pl.BlockSpec((1, r, LANES), lambda ei: (ei, 0, 0))],
        out_specs=[pl.BlockSpec((1, ncb, 1, LANES), lambda ei: (ei, 0, 0, 0)),
                   pl.BlockSpec((1, ncb, 1, LANES), lambda ei: (ei, 0, 0, 0))],
        out_shape=[jax.ShapeDtypeStruct((e, ncb, 1, LANES), jnp.int32),
                   jax.ShapeDtypeStruct((e, ncb, 1, LANES), F32)],
        scratch_shapes=[pltpu.VMEM((ncb + 2, COMPACT_ROWS, LANES), F32)],
        compiler_params=_cparams(1),
        name="route_compact",
    )(off, posm, affT3)
    return idx.reshape(e, cap), gate.reshape(e, cap, 1)


def _ffn_kernel(idx_ref, x_hbm, wg_ref, wu_ref, wd_ref, gate_ref, y_ref, xf_ref, xb_ref, acc_ref, sem, *, nf):
    f = pl.program_id(2)
    tc = xf_ref.shape[0]

    def row_copy(i):
        return pltpu.make_async_copy(x_hbm.at[pl.ds(idx_ref[0, 0, i], 1)], xf_ref.at[pl.ds(i, 1)], sem)

    @pl.when(f == 0)
    def _():
        def start(i, c):
            row_copy(i).start()
            return c

        def wait(i, c):
            row_copy(i).wait()
            return c

        lax.fori_loop(0, tc, start, 0, unroll=8)
        lax.fori_loop(0, tc, wait, 0, unroll=8)
        xb_ref[...] = xf_ref[...].astype(BF16)

    x = xb_ref[...]
    g = jnp.dot(x, wg_ref[0], preferred_element_type=F32)
    u = jnp.dot(x, wu_ref[0], preferred_element_type=F32)
    h = (g / (1.0 + jnp.exp(-g))) * u
    contrib = jnp.dot(h.astype(BF16), wd_ref[0], preferred_element_type=F32)

    @pl.when(f == 0)
    def _():
        acc_ref[...] = contrib

    @pl.when(f > 0)
    def _():
        acc_ref[...] += contrib

    @pl.when(f == nf - 1)
    def _():
        y_ref[0] = (acc_ref[...] * gate_ref[0]).astype(y_ref.dtype)


def _expert_ffn(idx, x2d, w_gu, w_dn, gates):
    e, c = idx.shape
    d = x2d.shape[1]
    fdim = w_dn.shape[1]
    tc = min(1024, c)
    ncb = c // tc
    tf = 256
    nf = fdim // tf
    kern = functools.partial(_ffn_kernel, nf=nf)
    return pl.pallas_call(
        kern,
        grid=(e, ncb, nf),
        in_specs=[
            pl.BlockSpec((1, 1, tc), lambda ei, ci, fi: (ei * ncb + ci, 0, 0), memory_space=pltpu.SMEM),
            pl.BlockSpec(memory_space=pl.ANY),
            pl.BlockSpec((1, d, tf), lambda ei, ci, fi: (ei, 0, fi)),
            pl.BlockSpec((1, d, tf), lambda ei, ci, fi: (ei, 0, fi + nf)),
            pl.BlockSpec((1, tf, d), lambda ei, ci, fi: (ei, fi, 0)),
            pl.BlockSpec((1, tc, 1), lambda ei, ci, fi: (ei, ci, 0)),
        ],
        out_specs=pl.BlockSpec((1, tc, d), lambda ei, ci, fi: (ei, ci, 0)),
        out_shape=jax.ShapeDtypeStruct((e, c, d), BF16),
        scratch_shapes=[pltpu.VMEM((tc, d), F32), pltpu.VMEM((tc, d), BF16), pltpu.VMEM((tc, d), F32),
                        pltpu.SemaphoreType.DMA(())],
        compiler_params=_cparams(3),
        name="expert_ffn",
    )(idx.reshape(e * ncb, 1, tc), x2d, w_gu, w_gu, w_dn, gates)


COMBINE_ROWS = 8
COMBINE_BUFS = 4
BF16_ROWS = 16
WIN = LANES + BF16_ROWS


def _combine_kernel(off_ref, x_ref, posm_ref, y_hbm, ln_ref, o_ref, ybuf, sem, *, alpha, cap):
    t = pl.program_id(0)
    n_exp = posm_ref.shape[0]
    slot = lax.broadcasted_iota(jnp.int32, (WIN, LANES), 0)

    def win_start(rr, e):
        start = lax.shift_right_logical(off_ref[e, t * COMBINE_ROWS + rr], 4) * BF16_ROWS
        return pl.multiple_of(jnp.minimum(start, cap - WIN), BF16_ROWS)

    def win_copy(rr, e):
        return pltpu.make_async_copy(y_hbm.at[e, pl.ds(win_start(rr, e), WIN)],
                                     ybuf.at[e % COMBINE_BUFS], sem.at[e % COMBINE_BUFS])

    for e in range(COMBINE_BUFS - 1):
        win_copy(0, e).start()

    def body(rr, carry):
        tot = jnp.zeros((LANES, x_ref.shape[1]), F32)
        for e in range(n_exp):
            nxt = e + COMBINE_BUFS - 1
            if nxt < n_exp:
                win_copy(rr, nxt).start()
            else:
                @pl.when(rr + 1 < COMBINE_ROWS)
                def _():
                    win_copy(rr + 1, nxt - n_exp).start()
            win_copy(rr, e).wait()
            rel = posm_ref[e, pl.ds(rr, 1), :] - win_start(rr, e)
            hit = (rel == slot).astype(BF16)
            tot = tot + lax.dot_general(hit, ybuf[e % COMBINE_BUFS], _TN, preferred_element_type=F32)
        rows = pl.ds(pl.multiple_of(rr * LANES, LANES), LANES)
        o_ref[rows, :] = _layer_norm(alpha * x_ref[rows, :] + tot, ln_ref[0:1, :], ln_ref[1:2, :])
        return carry

    lax.fori_loop(0, COMBINE_ROWS, body, 0)


def _combine_ln(off, x2d, posm, y, ln, alpha, cap):
    n, d = x2d.shape
    e, r, _ = posm.shape
    rows = COMBINE_ROWS
    assert r % rows == 0 and e % COMBINE_BUFS == 0 and cap >= WIN
    tt = rows * LANES
    return pl.pallas_call(
        functools.partial(_combine_kernel, alpha=alpha, cap=cap),
        grid=(n // tt,),
        in_specs=[pl.BlockSpec(memory_space=pltpu.SMEM),
                  pl.BlockSpec((tt, d), lambda ti: (ti, 0)),
                  pl.BlockSpec((e, rows, LANES), lambda ti: (0, ti, 0)),
                  pl.BlockSpec(memory_space=pl.ANY),
                  pl.BlockSpec((2, d), lambda ti: (0, 0))],
        out_specs=pl.BlockSpec((tt, d), lambda ti: (ti, 0)),
        out_shape=jax.ShapeDtypeStruct((n, d), F32),
        scratch_shapes=[pltpu.VMEM((COMBINE_BUFS, WIN, d), BF16), pltpu.SemaphoreType.DMA((COMBINE_BUFS,))],
        compiler_params=_cparams(1),
        name="combine_ln",
    )(off, x2d, posm, y, ln)


def _rope_tables(seq):
    def tab(pos, dim):
        inv = ROPE_THETA ** (-jnp.arange(0, dim, 2, dtype=F32) / dim)
        ang = pos.astype(F32)[:, None] * inv[None, :]
        return jnp.cos(ang), jnp.sin(ang)

    c1, s1 = tab(jnp.arange(seq), HEAD_DIM)
    cos1 = jnp.concatenate([c1, c1], axis=-1)
    sin1 = jnp.concatenate([-s1, s1], axis=-1)
    cr, sr = tab(jnp.arange(seq) // GRID_W, HEAD_DIM // 2)
    cc, sc = tab(jnp.arange(seq) % GRID_W, HEAD_DIM // 2)
    cos2 = jnp.concatenate([cr, cr, cc, cc], axis=-1)
    sin2 = jnp.concatenate([-sr, sr, -sc, sc], axis=-1)
    rep = LANES // HEAD_DIM
    nat = lambda c, s: (jnp.tile(c, (1, rep)), jnp.tile(s, (1, rep)))
    return ((cos1.T, sin1.T), nat(cos1, sin1)), ((cos2.T, sin2.T), nat(cos2, sin2))


def _prep_params(p):
    bf = lambda w: w.astype(BF16)
    tr = lambda w: jnp.swapaxes(w, -1, -2).astype(BF16)
    a, bw, cw = p["a_w_in"], p["b_w_in"], p["c_w_in"]
    out = dict(p)
    out.update(
        a_wqT=tr(a[:, :, 0:MIX_W]), a_wk=bf(a[:, :, MIX_W:2 * MIX_W]),
        a_wvT=tr(a[:, :, 2 * MIX_W:3 * MIX_W]), a_wmT=tr(a[:, :, 3 * MIX_W:]),
        w_memk=bf(p["w_mem_kv"][:, :, 0:MEM_W]), w_memvT=tr(p["w_mem_kv"][:, :, MEM_W:]),
        w_o=bf(p["w_o"]), w_routerT=tr(p["w_router"]),
        w_gate_up=bf(p["w_gate_up"]), w_down=bf(p["w_down"]),
    )
    for name, w in (("b", bw), ("c", cw)):
        out[name + "_wqT"] = tr(w[:, :, 0:MIX_W])
        out[name + "_wk"] = bf(w[:, :, MIX_W:MIX_W + KV_W])
        out[name + "_wvT"] = tr(w[:, :, MIX_W + KV_W:MIX_W + 2 * KV_W])
        out[name + "_wmT"] = tr(w[:, :, MIX_W + 2 * KV_W:])
    return out


def _encode(x, mem, p):
    b, s, d = x.shape
    n = b * s
    m_len = mem.shape[1]
    depth = p["w_o"].shape[0]
    alpha = (2 * depth) ** 0.25
    (tabs1T, tabs1), (tabs2T, tabs2) = _rope_tables(s)
    bd = (jnp.arange(LANES)[:, None] // HEAD_DIM == jnp.arange(LANES)[None, :] // HEAD_DIM).astype(BF16)
    ones_g = jnp.ones((1, LANES), F32)
    ones_gT = jnp.ones((HEAD_DIM, LANES), F32)
    cap = EC_FACTOR * n // N_EXPERTS
    x2d = x.reshape(n, d)
    mem2d = mem.reshape(b * m_len, d)
    tq_full, tk_full = 512, 512

    for i in range(depth):
        kind, j = i % N_MIXERS, i // N_MIXERS
        common = dict(batch=b, seq=s)
        if kind == 0:
            qT, k4, vT5, qmT = _inproj(x2d, p["a_wqT"][j], p["a_wvT"][j], p["a_wmT"][j], p["a_wk"][j],
                                       tabs1T, tabs1, ones_gT, ones_g, bd, half=HEAD_DIM // 2,
                                       qk_norm=False, vchunk=tk_full, **common)
            lam_init = 0.8 - 0.6 * math.exp(-0.3 * i)
            tk = min(tk_full, s)
            k5 = k4.reshape(b, 2 * A_HEADS, s // tk, tk, HEAD_DIM)
            g_col = jnp.broadcast_to(p["a_subln"][j][:, None], (A_V_DIM, LANES))
            mixT = _diff_attention(qT, k5, vT5, p["a_lambda"][j], g_col, lam_init, tq_full)
        elif kind == 1:
            qT, k4, vT5, qmT = _inproj(x2d, p["b_wqT"][j], p["b_wvT"][j], p["b_wmT"][j], p["b_wk"][j],
                                       tabs1T, tabs1, ones_gT, ones_g, bd, half=HEAD_DIM // 2,
                                       qk_norm=False, vchunk=LANES, **common)
            k5 = k4.reshape(b, GQA_KV_HEADS, s // LANES, LANES, HEAD_DIM)
            mixT = _window_attention(qT, k5, vT5, p["b_sink"][j], GQA_GROUP, tq_full)
        else:
            gq = jnp.broadcast_to(p["c_qk_norm"][j][0][:, None], (HEAD_DIM, LANES))
            gk = jnp.tile(p["c_qk_norm"][j][1], LANES // HEAD_DIM)[None, :]
            qT, k4, vT5, qmT = _inproj(x2d, p["c_wqT"][j], p["c_wvT"][j], p["c_wmT"][j], p["c_wk"][j],
                                       tabs2T, tabs2, gq, gk, bd, half=HEAD_DIM // 4,
                                       qk_norm=True, vchunk=tk_full, **common)
            tk = min(tk_full, s)
            k5 = k4.reshape(b, GQA_KV_HEADS, s // tk, tk, HEAD_DIM)
            mixT = _gqa_attention(qT, k5, vT5, GQA_GROUP, tq_full)

        km, vmT = _memproj(mem2d, p["w_memk"][i], p["w_memvT"][i], b, m_len)
        memT = _gqa_attention(qmT, km.reshape(b, MEM_HEADS, 1, m_len, HEAD_DIM), vmT, 1, 2 * tq_full)

        x2d, affT = _oproj(mixT, memT, x2d, p["w_o"][i], p["ln_mix"][i], p["w_routerT"][i], alpha, b, s)

        affT3 = affT.reshape(N_EXPERTS, n // LANES, LANES)
        posm, off = _route_select(affT3, cap)
        idx, gates = _route_compact(off, posm, affT3, cap)
        y = _expert_ffn(idx, x2d, p["w_gate_up"][i], p["w_down"][i], gates)
        x2d = _combine_ln(off, x2d, posm, y, p["ln_ffn"][i], alpha, cap)
    return x2d.reshape(b, s, d)


def kernel(x_prompt, x_sample, mem_prompt, mem_sample, a_w_in, a_lambda, a_subln, b_w_in, b_sink, c_w_in, c_qk_norm, w_mem_kv, w_o, ln_mix, w_router, w_gate_up, w_down, ln_ffn):
    p = _prep_params(dict(
        a_w_in=a_w_in, a_lambda=a_lambda, a_subln=a_subln, b_w_in=b_w_in, b_sink=b_sink,
        c_w_in=c_w_in, c_qk_norm=c_qk_norm, w_mem_kv=w_mem_kv, w_o=w_o, ln_mix=ln_mix,
        w_router=w_router, w_gate_up=w_gate_up, w_down=w_down, ln_ffn=ln_ffn))
    return (_encode(x_prompt, mem_prompt, p), _encode(x_sample, mem_sample, p))
```

```python
import functools
import math

import jax
import jax.numpy as jnp
from jax import lax
from jax.experimental import pallas as pl
from jax.experimental.pallas import tpu as pltpu

F32 = jnp.float32
BF16 = jnp.bfloat16

HEAD_DIM = 64
GQA_KV_HEADS = 4
GQA_GROUP = 3
A_HEADS = 6
A_V_DIM = 2 * HEAD_DIM
MEM_HEADS = 4
MIX_W = 768
MEM_W = MEM_HEADS * HEAD_DIM
KV_W = GQA_KV_HEADS * HEAD_DIM
N_MIXERS = 3
WINDOW = 128
GRID_W = 64
ROPE_THETA = 10000.0
N_EXPERTS = 16
EC_FACTOR = 2
LN_EPS = 1e-5
RMS_EPS = 1e-6
LOG2E = 1.4426950408889634
QK_SCALE = HEAD_DIM ** -0.5 * LOG2E
LANES = 128
ONES_ROWS = 16
VMEM_LIMIT = 52 * 1024 * 1024

_NT = (((1,), (1,)), ((), ()))
_TN = (((0,), (0,)), ((), ()))


def _cparams(grid_rank):
    return pltpu.CompilerParams(dimension_semantics=("arbitrary",) * grid_rank,
                                vmem_limit_bytes=VMEM_LIMIT)


def _layer_norm(h, g, b):
    mu = jnp.mean(h, axis=-1, keepdims=True)
    d = h - mu
    var = jnp.mean(d * d, axis=-1, keepdims=True)
    return d * lax.rsqrt(var + LN_EPS) * g + b


def _lane_tile(a, width):
    return jnp.concatenate([a] * (width // a.shape[1]), axis=1)


def _inproj_kernel(x_ref, wqT_ref, wvT_ref, wmT_ref, wk_ref, cosT_ref, sinT_ref, cos_ref, sin_ref,
                   gT_ref, g_ref, bd_ref, qT_ref, k_ref, vT_ref, qmT_ref, *, half, qk_norm, vchunk):
    xb = x_ref[...].astype(BF16)
    tm = xb.shape[0]
    cosT = cosT_ref[...]
    sinT = sinT_ref[...]
    nparts = HEAD_DIM // half

    yqT = lax.dot_general(wqT_ref[...], xb, _NT, preferred_element_type=F32)
    for h in range(yqT.shape[0] // HEAD_DIM):
        xh = yqT[h * HEAD_DIM:(h + 1) * HEAD_DIM]
        if qk_norm:
            ms = jnp.mean(xh * xh, axis=0, keepdims=True)
            xh = xh * lax.rsqrt(ms + RMS_EPS) * _lane_tile(gT_ref[...], tm)
        rot = jnp.concatenate([xh[(p ^ 1) * half:((p ^ 1) + 1) * half] for p in range(nparts)], axis=0)
        qT_ref[0, h * HEAD_DIM:(h + 1) * HEAD_DIM, :] = ((xh * cosT + rot * sinT) * QK_SCALE).astype(BF16)

    yvT = lax.dot_general(wvT_ref[...], xb, _NT, preferred_element_type=F32).astype(BF16)
    for c in range(tm // vchunk):
        vT_ref[0, c] = yvT[:, c * vchunk:(c + 1) * vchunk]
    ymT = lax.dot_general(wmT_ref[...], xb, _NT, preferred_element_type=F32)
    qmT_ref[0] = (ymT * QK_SCALE).astype(BF16)

    lane = lax.broadcasted_iota(jnp.int32, (tm, LANES), 1)
    first = (lane % (2 * half)) < half
    cos = cos_ref[...]
    sin = sin_ref[...]
    yk = jnp.dot(xb, wk_ref[...], preferred_element_type=F32)
    for c in range(yk.shape[1] // LANES):
        ch = yk[:, c * LANES:(c + 1) * LANES]
        if qk_norm:
            sq = ch * ch
            hi = sq.astype(BF16)
            lo = (sq - hi.astype(F32)).astype(BF16)
            ss = (jnp.dot(hi, bd_ref[...], preferred_element_type=F32)
                  + jnp.dot(lo, bd_ref[...], preferred_element_type=F32))
            ch = ch * lax.rsqrt(ss * (1.0 / HEAD_DIM) + RMS_EPS) * g_ref[...]
        rot = jnp.where(first, pltpu.roll(ch, LANES - half, 1), pltpu.roll(ch, half, 1))
        ch = (ch * cos + rot * sin).astype(BF16)
        k_ref[0, 2 * c] = ch[:, 0:HEAD_DIM]
        k_ref[0, 2 * c + 1] = ch[:, HEAD_DIM:LANES]


def _inproj(x2d, wqT, wvT, wmT, wk, tabsT, tabs, gT, g, bd, *, batch, seq, half, qk_norm, vchunk):
    n, d = x2d.shape
    tm = min(512, seq)
    vchunk = min(vchunk, tm)
    nb = seq // tm
    qw, vw, kw = wqT.shape[0], wvT.shape[0], wk.shape[1]
    kern = functools.partial(_inproj_kernel, half=half, qk_norm=qk_norm, vchunk=vchunk)
    const = lambda b, i: (0, 0)
    return pl.pallas_call(
        kern,
        grid=(batch, nb),
        in_specs=[
            pl.BlockSpec((tm, d), lambda b, i: (b * nb + i, 0)),
            pl.BlockSpec(wqT.shape, const),
            pl.BlockSpec(wvT.shape, const),
            pl.BlockSpec(wmT.shape, const),
            pl.BlockSpec(wk.shape, const),
            pl.BlockSpec((HEAD_DIM, tm), lambda b, i: (0, i)),
            pl.BlockSpec((HEAD_DIM, tm), lambda b, i: (0, i)),
            pl.BlockSpec((tm, LANES), lambda b, i: (i, 0)),
            pl.BlockSpec((tm, LANES), lambda b, i: (i, 0)),
            pl.BlockSpec((HEAD_DIM, LANES), const),
            pl.BlockSpec((1, LANES), const),
            pl.BlockSpec((LANES, LANES), const),
        ],
        out_specs=[
            pl.BlockSpec((1, qw, tm), lambda b, i: (b, 0, i)),
            pl.BlockSpec((1, kw // HEAD_DIM, tm, HEAD_DIM), lambda b, i: (b, 0, i, 0)),
            pl.BlockSpec((1, tm // vchunk, vw, vchunk), lambda b, i: (b, i, 0, 0)),
            pl.BlockSpec((1, MEM_W, tm), lambda b, i: (b, 0, i)),
        ],
        out_shape=[
            jax.ShapeDtypeStruct((batch, qw, seq), BF16),
            jax.ShapeDtypeStruct((batch, kw // HEAD_DIM, seq, HEAD_DIM), BF16),
            jax.ShapeDtypeStruct((batch, seq // vchunk, vw, vchunk), BF16),
            jax.ShapeDtypeStruct((batch, MEM_W, seq), BF16),
        ],
        compiler_params=_cparams(2),
        name="inproj",
    )(x2d, wqT, wvT, wmT, wk, *tabsT, *tabs, gT, g, bd)


def _memproj_kernel(x_ref, wk_ref, wvT_ref, k_ref, vT_ref):
    xb = x_ref[...].astype(BF16)
    yk = jnp.dot(xb, wk_ref[...], preferred_element_type=F32).astype(BF16)
    for h in range(MEM_HEADS):
        k_ref[0, h] = yk[:, h * HEAD_DIM:(h + 1) * HEAD_DIM]
    vT_ref[0, 0] = lax.dot_general(wvT_ref[...], xb, _NT, preferred_element_type=F32).astype(BF16)


def _memproj(mem2d, wk, wvT, batch, m_len):
    d = mem2d.shape[1]
    return pl.pallas_call(
        _memproj_kernel,
        grid=(batch,),
        in_specs=[pl.BlockSpec((m_len, d), lambda b: (b, 0)),
                  pl.BlockSpec(wk.shape, lambda b: (0, 0)),
                  pl.BlockSpec(wvT.shape, lambda b: (0, 0))],
        out_specs=[pl.BlockSpec((1, MEM_HEADS, m_len, HEAD_DIM), lambda b: (b, 0, 0, 0)),
                   pl.BlockSpec((1, 1, MEM_W, m_len), lambda b: (b, 0, 0, 0))],
        out_shape=[jax.ShapeDtypeStruct((batch, MEM_HEADS, m_len, HEAD_DIM), BF16),
                   jax.ShapeDtypeStruct((batch, 1, MEM_W, m_len), BF16)],
        compiler_params=_cparams(1),
        name="memproj",
    )(mem2d, wk, wvT)


def _ones_rows(tk):
    return (lax.broadcasted_iota(jnp.int32, (ONES_ROWS, tk), 0) == 0).astype(BF16)


def _flash_step(sT, va, m, acc):
    m_new = jnp.maximum(m, jnp.max(sT, axis=0, keepdims=True))
    a = jnp.exp2(m - m_new)
    pT = jnp.exp2(sT - m_new).astype(BF16)
    return m_new, a * acc + jnp.dot(va, pT, preferred_element_type=F32)


def _gqa_kernel(qT_ref, k_ref, vT_ref, o_ref, *, group, nk, unroll):
    tk = k_ref.shape[3]
    qT = jnp.concatenate([qT_ref[0, g * HEAD_DIM:(g + 1) * HEAD_DIM, :] for g in range(group)], axis=1)
    nq = qT.shape[1]
    tq = nq // group
    ones = _ones_rows(tk)

    def body(j, carry):
        sT = jnp.dot(k_ref[0, 0, j], qT, preferred_element_type=F32)
        va = jnp.concatenate([vT_ref[0, j], ones], axis=0)
        return _flash_step(sT, va, *carry)

    init = (jnp.full((1, nq), -jnp.inf, F32), jnp.zeros((HEAD_DIM + ONES_ROWS, nq), F32))
    _, acc = lax.fori_loop(0, nk, body, init, unroll=unroll)
    o = acc[0:HEAD_DIM] / acc[HEAD_DIM:HEAD_DIM + 1]
    for g in range(group):
        o_ref[0, g * HEAD_DIM:(g + 1) * HEAD_DIM, :] = o[:, g * tq:(g + 1) * tq].astype(o_ref.dtype)


def _gqa_attention(qT, k5, vT5, group, tq):
    b, qrows, s = qT.shape
    _, kvh, nk, tk, _ = k5.shape
    tq = min(tq, s)
    kern = functools.partial(_gqa_kernel, group=group, nk=nk, unroll=math.gcd(nk, 4))
    return pl.pallas_call(
        kern,
        grid=(b, kvh, s // tq),
        in_specs=[
            pl.BlockSpec((1, group * HEAD_DIM, tq), lambda bi, h, i: (bi, h, i)),
            pl.BlockSpec((1, 1, nk, tk, HEAD_DIM), lambda bi, h, i: (bi, h, 0, 0, 0)),
            pl.BlockSpec((1, nk, HEAD_DIM, tk), lambda bi, h, i: (bi, 0, h, 0)),
        ],
        out_specs=pl.BlockSpec((1, group * HEAD_DIM, tq), lambda bi, h, i: (bi, h, i)),
        out_shape=jax.ShapeDtypeStruct(qT.shape, BF16),
        compiler_params=_cparams(3),
        name="gqa_attention",
    )(qT, k5, vT5)


def _diff_kernel(lam_ref, g_ref, qT_ref, k_ref, vT_ref, o_ref, *, nk, unroll, lam_init):
    tk = k_ref.shape[3]
    tq = qT_ref.shape[2]
    lp = lam_ref[...]
    lam = (jnp.exp(jnp.sum(lp[0:1] * lp[1:2], axis=-1, keepdims=True))
           - jnp.exp(jnp.sum(lp[2:3] * lp[3:4], axis=-1, keepdims=True)) + lam_init)
    q0T = qT_ref[0, 0:HEAD_DIM, :]
    q1T = qT_ref[0, HEAD_DIM:2 * HEAD_DIM, :]
    ones = _ones_rows(tk)

    def body(j, carry):
        sT = jnp.concatenate([jnp.dot(k_ref[0, 0, j], q0T, preferred_element_type=F32),
                              jnp.dot(k_ref[0, 1, j], q1T, preferred_element_type=F32)], axis=1)
        va = jnp.concatenate([vT_ref[0, j], ones], axis=0)
        return _flash_step(sT, va, *carry)

    init = (jnp.full((1, 2 * tq), -jnp.inf, F32), jnp.zeros((A_V_DIM + ONES_ROWS, 2 * tq), F32))
    _, acc = lax.fori_loop(0, nk, body, init, unroll=unroll)
    o = acc[0:A_V_DIM] / acc[A_V_DIM:A_V_DIM + 1]
    o = o[:, 0:tq] - lam * o[:, tq:2 * tq]
    ms = jnp.mean(o * o, axis=0, keepdims=True)
    o = o * lax.rsqrt(ms + RMS_EPS) * _lane_tile(g_ref[...], tq) * (1.0 - lam_init)
    o_ref[0] = o.astype(o_ref.dtype)


def _diff_attention(qT, k5, vT5, lam_p, g_col, lam_init, tq):
    b, qrows, s = qT.shape
    _, _, nk, tk, _ = k5.shape
    tq = min(tq, s)
    kern = functools.partial(_diff_kernel, nk=nk, unroll=math.gcd(nk, 4), lam_init=lam_init)
    return pl.pallas_call(
        kern,
        grid=(b, A_HEADS, s // tq),
        in_specs=[
            pl.BlockSpec((4, HEAD_DIM), lambda bi, h, i: (0, 0)),
            pl.BlockSpec((A_V_DIM, LANES), lambda bi, h, i: (0, 0)),
            pl.BlockSpec((1, 2 * HEAD_DIM, tq), lambda bi, h, i: (bi, h, i)),
            pl.BlockSpec((1, 2, nk, tk, HEAD_DIM), lambda bi, h, i: (bi, h, 0, 0, 0)),
            pl.BlockSpec((1, nk, A_V_DIM, tk), lambda bi, h, i: (bi, 0, h, 0)),
        ],
        out_specs=pl.BlockSpec((1, A_V_DIM, tq), lambda bi, h, i: (bi, h, i)),
        out_shape=jax.ShapeDtypeStruct(qT.shape, BF16),
        compiler_params=_cparams(3),
        name="diff_attention",
    )(lam_p, g_col, qT, k5, vT5)


def _window_kernel(sink_ref, qT_ref, k_ref, vT_ref, o_ref, *, group, nwc, nchunks):
    h = pl.program_id(1)
    tq = qT_ref.shape[2]
    nq = group * tq
    q0 = pl.program_id(2) * tq
    c0 = jnp.clip(q0 // LANES - WINDOW // LANES, 0, nchunks - nwc)
    qT = jnp.concatenate([qT_ref[0, g * HEAD_DIM:(g + 1) * HEAD_DIM, :] for g in range(group)], axis=1)
    qpos = q0 + lax.broadcasted_iota(jnp.int32, (LANES, nq), 1) % tq
    krow = lax.broadcasted_iota(jnp.int32, (LANES, nq), 0)
    sink = jnp.concatenate(
        [jnp.full((1, tq), sink_ref[h * group + g] * LOG2E, F32) for g in range(group)], axis=1)
    ones = _ones_rows(LANES)
    scores = []
    m = sink
    for c in range(nwc):
        sT = jnp.dot(k_ref[0, 0, c0 + c], qT, preferred_element_type=F32)
        kpos = (c0 + c) * LANES + krow
        sT = jnp.where(jnp.abs(kpos - qpos) <= WINDOW, sT, -jnp.inf)
        scores.append(sT)
        m = jnp.maximum(m, jnp.max(sT, axis=0, keepdims=True))
    acc = jnp.zeros((HEAD_DIM + ONES_ROWS, nq), F32)
    for c in range(nwc):
        va = jnp.concatenate([vT_ref[0, c0 + c], ones], axis=0)
        acc = acc + jnp.dot(va, jnp.exp2(scores[c] - m).astype(BF16), preferred_element_type=F32)
    o = acc[0:HEAD_DIM] / (acc[HEAD_DIM:HEAD_DIM + 1] + jnp.exp2(sink - m))
    for g in range(group):
        o_ref[0, g * HEAD_DIM:(g + 1) * HEAD_DIM, :] = o[:, g * tq:(g + 1) * tq].astype(o_ref.dtype)


def _window_attention(qT, k5, vT5, sink, group, tq):
    b, qrows, s = qT.shape
    _, kvh, nchunks, tk, _ = k5.shape
    tq = min(tq, s)
    nwc = min(nchunks, tq // LANES + 2 * (WINDOW // LANES))
    kern = functools.partial(_window_kernel, group=group, nwc=nwc, nchunks=nchunks)
    return pl.pallas_call(
        kern,
        grid=(b, kvh, s // tq),
        in_specs=[
            pl.BlockSpec(memory_space=pltpu.SMEM),
            pl.BlockSpec((1, group * HEAD_DIM, tq), lambda bi, h, i: (bi, h, i)),
            pl.BlockSpec((1, 1, nchunks, tk, HEAD_DIM), lambda bi, h, i: (bi, h, 0, 0, 0)),
            pl.BlockSpec((1, nchunks, HEAD_DIM, tk), lambda bi, h, i: (bi, 0, h, 0)),
        ],
        out_specs=pl.BlockSpec((1, group * HEAD_DIM, tq), lambda bi, h, i: (bi, h, i)),
        out_shape=jax.ShapeDtypeStruct(qT.shape, BF16),
        compiler_params=_cparams(3),
        name="window_attention",
    )(sink, qT, k5, vT5)


def _oproj_kernel(mixT_ref, memT_ref, x_ref, wo_ref, ln_ref, wrT_ref, xo_ref, affT_ref, *, alpha):
    sub = (lax.dot_general(mixT_ref[0], wo_ref[0:MIX_W, :], _TN, preferred_element_type=F32)
           + lax.dot_general(memT_ref[0], wo_ref[MIX_W:MIX_W + MEM_W, :], _TN,
                             preferred_element_type=F32))
    y = _layer_norm(alpha * x_ref[...] + sub, ln_ref[0:1, :], ln_ref[1:2, :])
    xo_ref[...] = y
    logitsT = lax.dot_general(wrT_ref[...], y.astype(BF16), _NT, preferred_element_type=F32)
    e = jnp.exp(logitsT - jnp.max(logitsT, axis=0, keepdims=True))
    affT_ref[...] = e / jnp.sum(e, axis=0, keepdims=True)


def _oproj(mixT, memT, x2d, wo, ln, wrT, alpha, batch, seq):
    n, d = x2d.shape
    tm = min(512, seq)
    nb = seq // tm
    kern = functools.partial(_oproj_kernel, alpha=alpha)
    const = lambda b, i: (0, 0)
    row = lambda b, i: (b * nb + i, 0)
    return pl.pallas_call(
        kern,
        grid=(batch, nb),
        in_specs=[
            pl.BlockSpec((1, MIX_W, tm), lambda b, i: (b, 0, i)),
            pl.BlockSpec((1, MEM_W, tm), lambda b, i: (b, 0, i)),
            pl.BlockSpec((tm, d), row),
            pl.BlockSpec(wo.shape, const),
            pl.BlockSpec((2, d), const),
            pl.BlockSpec(wrT.shape, const),
        ],
        out_specs=[
            pl.BlockSpec((tm, d), row),
            pl.BlockSpec((N_EXPERTS, tm), lambda b, i: (0, b * nb + i)),
        ],
        out_shape=[
            jax.ShapeDtypeStruct((n, d), F32),
            jax.ShapeDtypeStruct((N_EXPERTS, n), F32),
        ],
        compiler_params=_cparams(2),
        name="oproj_ln_router",
    )(mixT, memT, x2d, wo, ln, wrT)


def _route_select_kernel(affT_ref, tri_ref, ones_ref, low_ref, posm_ref, off_ref, *, cap):
    a = affT_ref[0]
    bits = lax.bitcast_convert_type(a, jnp.int32)

    def count(mask):
        return jnp.sum(jnp.sum(mask.astype(F32), axis=1, keepdims=True), axis=0, keepdims=True)

    def bisect(i, t):
        cand = t | jnp.left_shift(jnp.int32(1), 30 - i)
        return jnp.where(count(bits >= cand) >= cap, cand, t)

    t = lax.fori_loop(0, 31, bisect, jnp.zeros((1, 1), jnp.int32))

    def excl_cumsum(mask):
        mb = mask.astype(BF16)
        incl = jnp.dot(mb, tri_ref[...], preferred_element_type=F32)
        tot = jnp.dot(mb, ones_ref[...], preferred_element_type=F32)
        before = jnp.dot(low_ref[...], tot.astype(BF16), preferred_element_type=F32)
        return before + incl - mask.astype(F32), before

    gt = bits > t
    eq = bits == t
    need = cap - count(gt)
    rank, _ = excl_cumsum(eq)
    sel = gt | (eq & (rank < need))
    pos, before = excl_cumsum(sel)
    posm_ref[0] = jnp.where(sel, pos, -1.0).astype(jnp.int32)
    off_ref[0] = before[:, 0:1].astype(jnp.int32)


def _route_select(affT3, cap):
    e, r, _ = affT3.shape
    tri = (jnp.arange(LANES)[:, None] <= jnp.arange(LANES)[None, :]).astype(BF16)
    ones = jnp.ones((LANES, LANES), BF16)
    low = (jnp.arange(r)[None, :] < jnp.arange(r)[:, None]).astype(BF16)
    posm, off = pl.pallas_call(
        functools.partial(_route_select_kernel, cap=cap),
        grid=(e,),
        in_specs=[pl.BlockSpec((1, r, LANES), lambda ei: (ei, 0, 0)),
                  pl.BlockSpec((LANES, LANES), lambda ei: (0, 0)),
                  pl.BlockSpec((LANES, LANES), lambda ei: (0, 0)),
                  pl.BlockSpec((r, r), lambda ei: (0, 0))],
        out_specs=[pl.BlockSpec((1, r, LANES), lambda ei: (ei, 0, 0)),
                   pl.BlockSpec((1, r, 1), lambda ei: (ei, 0, 0))],
        out_shape=[jax.ShapeDtypeStruct((e, r, LANES), jnp.int32),
                   jax.ShapeDtypeStruct((e, r, 1), jnp.int32)],
        compiler_params=_cparams(1),
        name="route_select",
    )(affT3, tri, ones, low)
    return posm, off.reshape(e, r)


COMPACT_ROWS = 16


def _route_compact_kernel(off_ref, posm_ref, affT_ref, idx_ref, gate_ref, acc_ref, *, nrows, ncb):
    e = pl.program_id(0)
    acc_ref[...] = jnp.zeros_like(acc_ref)
    slot = lax.broadcasted_iota(jnp.int32, (LANES, LANES), 0)
    sub = lax.broadcasted_iota(jnp.int32, (COMPACT_ROWS, LANES), 0)
    lane = lax.broadcasted_iota(jnp.int32, (COMPACT_ROWS, LANES), 1).astype(F32)

    def body(r, carry):
        cb = lax.shift_right_logical(off_ref[e, r], 7)
        rel = posm_ref[0, pl.ds(r, 1), :] - cb * LANES
        a = affT_ref[0, pl.ds(r, 1), :]
        hi = a.astype(BF16).astype(F32)
        mid = (a - hi).astype(BF16).astype(F32)
        lo = a - hi - mid
        vals = jnp.where(sub == 0, lane, jnp.where(sub == 1, jnp.asarray(r, F32), jnp.where(
            sub == 2, hi, jnp.where(sub == 3, mid, jnp.where(sub == 4, lo, 0.0))))).astype(BF16)
        for part in range(2):
            hit = ((rel - part * LANES) == slot).astype(BF16)
            acc_ref[cb + part] += lax.dot_general(vals, hit, _NT, preferred_element_type=F32)
        return carry

    lax.fori_loop(0, nrows, body, 0, unroll=8)
    acc = acc_ref[0:ncb]
    idx_ref[0] = (acc[:, 1:2, :] * LANES + acc[:, 0:1, :]).astype(jnp.int32)
    gate_ref[0] = acc[:, 2:3, :] + acc[:, 3:4, :] + acc[:, 4:5, :]


def _route_compact(off, posm, affT3, cap):
    e, r, _ = posm.shape
    ncb = cap // LANES
    idx, gate = pl.pallas_call(
        functools.partial(_route_compact_kernel, nrows=r, ncb=ncb),
        grid=(e,),
        in_specs=[pl.BlockSpec(memory_space=pltpu.SMEM),
                  pl.BlockSpec((1, r, LANES), lambda ei: (ei, 0, 0)),
                  pl.BlockSpec((1, r, LANES), lambda ei: (ei, 0, 0))],
        out_specs=[pl.BlockSpec((1, ncb, 1, LANES), lambda ei: (ei, 0, 0, 0)),
                   pl.BlockSpec((1, ncb, 1, LANES), lambda ei: (ei, 0, 0, 0))],
        out_shape=[jax.ShapeDtypeStruct((e, ncb, 1, LANES), jnp.int32),
                   jax.ShapeDtypeStruct((e, ncb, 1, LANES), F32)],
        scratch_shapes=[pltpu.VMEM((ncb + 2, COMPACT_ROWS, LANES), F32)],
        compiler_params=_cparams(1),
        name="route_compact",
    )(off, posm, affT3)
    return idx.reshape(e, cap), gate.reshape(e, cap, 1)


FFN_ROWS = 512


def _ffn_kernel(idx_ref, nxt_ref, x_hbm, wg_ref, wu_ref, wd_ref, gate_ref, y_ref,
                xf_ref, xb_ref, acc_ref, sem, *, nf, nblk):
    f = pl.program_id(2)
    blk = pl.program_id(0) * pl.num_programs(1) + pl.program_id(1)
    slot = blk % 2
    tc = xb_ref.shape[0]
    chunk = min(FFN_ROWS, tc)

    def row_copy(ids, i, s):
        return pltpu.make_async_copy(x_hbm.at[pl.ds(ids[0, 0, i], 1)], xf_ref.at[s, pl.ds(i, 1)], sem.at[s])

    def start_rows(ids, s):
        def start(i, c):
            row_copy(ids, i, s).start()
            return c
        lax.fori_loop(0, tc, start, 0, unroll=8)

    @pl.when(f == 0)
    def _():
        @pl.when(blk == 0)
        def _():
            start_rows(idx_ref, slot)

        def wait(i, c):
            row_copy(idx_ref, i, slot).wait()
            return c
        lax.fori_loop(0, tc, wait, 0, unroll=8)
        for c in range(tc // chunk):
            rows = slice(c * chunk, (c + 1) * chunk)
            xb_ref[rows, :] = xf_ref[slot, rows, :].astype(BF16)
        acc_ref[...] = jnp.zeros_like(acc_ref)

    @pl.when((f == 1) & (blk + 1 < nblk))
    def _():
        start_rows(nxt_ref, 1 - slot)

    wg = wg_ref[0].astype(BF16)
    wu = wu_ref[0].astype(BF16)
    wd = wd_ref[0].astype(BF16)
    for c in range(tc // chunk):
        rows = slice(c * chunk, (c + 1) * chunk)
        x = xb_ref[rows, :]
        g = jnp.dot(x, wg, preferred_element_type=F32)
        u = jnp.dot(x, wu, preferred_element_type=F32)
        h = (g / (1.0 + jnp.exp(-g))) * u
        acc_ref[rows, :] += jnp.dot(h.astype(BF16), wd, preferred_element_type=F32)

    @pl.when(f == nf - 1)
    def _():
        y_ref[0] = (acc_ref[...] * gate_ref[0]).astype(y_ref.dtype)


def _expert_ffn(idx, x2d, w_gu, w_dn, gates):
    e, c = idx.shape
    d = x2d.shape[1]
    fdim = w_dn.shape[1]
    tc = min(2048, c)
    ncb = c // tc
    nblk = e * ncb
    tf = 256
    nf = fdim // tf
    assert nf >= 2 and tc % min(FFN_ROWS, tc) == 0
    kern = functools.partial(_ffn_kernel, nf=nf, nblk=nblk)
    idx3 = idx.reshape(nblk, 1, tc)
    return pl.pallas_call(
        kern,
        grid=(e, ncb, nf),
        in_specs=[
            pl.BlockSpec((1, 1, tc), lambda ei, ci, fi: (ei * ncb + ci, 0, 0), memory_space=pltpu.SMEM),
            pl.BlockSpec((1, 1, tc), lambda ei, ci, fi: (jnp.minimum(ei * ncb + ci + 1, nblk - 1), 0, 0),
                         memory_space=pltpu.SMEM),
            pl.BlockSpec(memory_space=pl.ANY),
            pl.BlockSpec((1, d, tf), lambda ei, ci, fi: (ei, 0, fi)),
            pl.BlockSpec((1, d, tf), lambda ei, ci, fi: (ei, 0, fi + nf)),
            pl.BlockSpec((1, tf, d), lambda ei, ci, fi: (ei, fi, 0)),
            pl.BlockSpec((1, tc, 1), lambda ei, ci, fi: (ei, ci, 0)),
        ],
        out_specs=pl.BlockSpec((1, tc, d), lambda ei, ci, fi: (ei, ci, 0)),
        out_shape=jax.ShapeDtypeStruct((e, c, d), BF16),
        scratch_shapes=[pltpu.VMEM((2, tc, d), F32), pltpu.VMEM((tc, d), BF16), pltpu.VMEM((tc, d), F32),
                        pltpu.SemaphoreType.DMA((2,))],
        compiler_params=_cparams(3),
        name="expert_ffn",
    )(idx3, idx3, x2d, w_gu, w_gu, w_dn, gates)


COMBINE_ROWS = 8
BF16_ROWS = 16
WIN = LANES + BF16_ROWS
WIN_SHORT = 3 * BF16_ROWS


def _combine_kernel(off_ref, x_ref, posm_ref, y_hbm, ln_ref, o_ref, ybuf, sem, *, alpha, cap):
    t = pl.program_id(0)
    n_exp = posm_ref.shape[0]
    d = x_ref.shape[1]
    slot = lax.broadcasted_iota(jnp.int32, (WIN, LANES), 0)

    @pl.when(t == 0)
    def _():
        ybuf[...] = jnp.zeros_like(ybuf)

    def win_start(rr, e):
        start = lax.shift_right_logical(off_ref[e, t * COMBINE_ROWS + rr], 4) * BF16_ROWS
        return pl.multiple_of(jnp.minimum(start, cap - WIN), BF16_ROWS)

    def for_each_window(rr, par, fn):
        for e in range(n_exp):
            start = win_start(rr, e)
            short = off_ref[e, t * COMBINE_ROWS + rr + 1] - start <= WIN_SHORT
            for cond, rows in ((short, WIN_SHORT), (jnp.logical_not(short), WIN)):
                @pl.when(cond)
                def _():
                    fn(pltpu.make_async_copy(y_hbm.at[e, pl.ds(start, rows)],
                                             ybuf.at[par, e, pl.ds(0, rows)], sem.at[par]))

    def row(rr, par):
        @pl.when(rr + 1 < COMBINE_ROWS)
        def _():
            for_each_window(rr + 1, 1 - par, lambda c: c.start())
        for_each_window(rr, par, lambda c: c.wait())
        hit = jnp.concatenate(
            [((posm_ref[e, pl.ds(rr, 1), :] - win_start(rr, e)) == slot).astype(BF16) for e in range(n_exp)],
            axis=0)
        tot = lax.dot_general(hit, ybuf[par].reshape(n_exp * WIN, d), _TN, preferred_element_type=F32)
        rows = pl.ds(pl.multiple_of(rr * LANES, LANES), LANES)
        o_ref[rows, :] = _layer_norm(alpha * x_ref[rows, :] + tot, ln_ref[0:1, :], ln_ref[1:2, :])

    for_each_window(0, 0, lambda c: c.start())

    def body(k, carry):
        row(2 * k, 0)
        row(2 * k + 1, 1)
        return carry

    lax.fori_loop(0, COMBINE_ROWS // 2, body, 0)


def _combine_ln(off, x2d, posm, y, ln, alpha, cap):
    n, d = x2d.shape
    e, r, _ = posm.shape
    rows = COMBINE_ROWS
    assert r % rows == 0 and rows % 2 == 0 and cap >= WIN
    tt = rows * LANES
    off = jnp.concatenate([off, jnp.full((e, 1), cap, jnp.int32)], axis=1)
    return pl.pallas_call(
        functools.partial(_combine_kernel, alpha=alpha, cap=cap),
        grid=(n // tt,),
        in_specs=[pl.BlockSpec(memory_space=pltpu.SMEM),
                  pl.BlockSpec((tt, d), lambda ti: (ti, 0)),
                  pl.BlockSpec((e, rows, LANES), lambda ti: (0, ti, 0)),
                  pl.BlockSpec(memory_space=pl.ANY),
                  pl.BlockSpec((2, d), lambda ti: (0, 0))],
        out_specs=pl.BlockSpec((tt, d), lambda ti: (ti, 0)),
        out_shape=jax.ShapeDtypeStruct((n, d), F32),
        scratch_shapes=[pltpu.VMEM((2, e, WIN, d), BF16), pltpu.SemaphoreType.DMA((2,))],
        compiler_params=_cparams(1),
        name="combine_ln",
    )(off, x2d, posm, y, ln)


def _rope_tables(seq):
    def tab(pos, dim):
        inv = ROPE_THETA ** (-jnp.arange(0, dim, 2, dtype=F32) / dim)
        ang = pos.astype(F32)[:, None] * inv[None, :]
        return jnp.cos(ang), jnp.sin(ang)

    c1, s1 = tab(jnp.arange(seq), HEAD_DIM)
    cos1 = jnp.concatenate([c1, c1], axis=-1)
    sin1 = jnp.concatenate([-s1, s1], axis=-1)
    cr, sr = tab(jnp.arange(seq) // GRID_W, HEAD_DIM // 2)
    cc, sc = tab(jnp.arange(seq) % GRID_W, HEAD_DIM // 2)
    cos2 = jnp.concatenate([cr, cr, cc, cc], axis=-1)
    sin2 = jnp.concatenate([-sr, sr, -sc, sc], axis=-1)
    rep = LANES // HEAD_DIM
    nat = lambda c, s: (jnp.tile(c, (1, rep)), jnp.tile(s, (1, rep)))
    return ((cos1.T, sin1.T), nat(cos1, sin1)), ((cos2.T, sin2.T), nat(cos2, sin2))


def _prep_params(p):
    bf = lambda w: w.astype(BF16)
    tr = lambda w: jnp.swapaxes(w, -1, -2).astype(BF16)
    a, bw, cw = p["a_w_in"], p["b_w_in"], p["c_w_in"]
    out = dict(p)
    out.update(
        a_wqT=tr(a[:, :, 0:MIX_W]), a_wk=bf(a[:, :, MIX_W:2 * MIX_W]),
        a_wvT=tr(a[:, :, 2 * MIX_W:3 * MIX_W]), a_wmT=tr(a[:, :, 3 * MIX_W:]),
        w_memk=bf(p["w_mem_kv"][:, :, 0:MEM_W]), w_memvT=tr(p["w_mem_kv"][:, :, MEM_W:]),
        w_o=bf(p["w_o"]), w_routerT=tr(p["w_router"]),
    )
    for name, w in (("b", bw), ("c", cw)):
        out[name + "_wqT"] = tr(w[:, :, 0:MIX_W])
        out[name + "_wk"] = bf(w[:, :, MIX_W:MIX_W + KV_W])
        out[name + "_wvT"] = tr(w[:, :, MIX_W + KV_W:MIX_W + 2 * KV_W])
        out[name + "_wmT"] = tr(w[:, :, MIX_W + 2 * KV_W:])
    return out


def _encode(x, mem, p):
    b, s, d = x.shape
    n = b * s
    m_len = mem.shape[1]
    depth = p["w_o"].shape[0]
    alpha = (2 * depth) ** 0.25
    (tabs1T, tabs1), (tabs2T, tabs2) = _rope_tables(s)
    bd = (jnp.arange(LANES)[:, None] // HEAD_DIM == jnp.arange(LANES)[None, :] // HEAD_DIM).astype(BF16)
    ones_g = jnp.ones((1, LANES), F32)
    ones_gT = jnp.ones((HEAD_DIM, LANES), F32)
    cap = EC_FACTOR * n // N_EXPERTS
    x2d = x.reshape(n, d)
    mem2d = mem.reshape(b * m_len, d)
    tq_full, tk_full = 512, 512

    for i in range(depth):
        kind, j = i % N_MIXERS, i // N_MIXERS
        common = dict(batch=b, seq=s)
        if kind == 0:
            qT, k4, vT5, qmT = _inproj(x2d, p["a_wqT"][j], p["a_wvT"][j], p["a_wmT"][j], p["a_wk"][j],
                                       tabs1T, tabs1, ones_gT, ones_g, bd, half=HEAD_DIM // 2,
                                       qk_norm=False, vchunk=tk_full, **common)
            lam_init = 0.8 - 0.6 * math.exp(-0.3 * i)
            tk = min(tk_full, s)
            k5 = k4.reshape(b, 2 * A_HEADS, s // tk, tk, HEAD_DIM)
            g_col = jnp.broadcast_to(p["a_subln"][j][:, None], (A_V_DIM, LANES))
            mixT = _diff_attention(qT, k5, vT5, p["a_lambda"][j], g_col, lam_init, 2 * tq_full)
        elif kind == 1:
            qT, k4, vT5, qmT = _inproj(x2d, p["b_wqT"][j], p["b_wvT"][j], p["b_wmT"][j], p["b_wk"][j],
                                       tabs1T, tabs1, ones_gT, ones_g, bd, half=HEAD_DIM // 2,
                                       qk_norm=False, vchunk=LANES, **common)
            k5 = k4.reshape(b, GQA_KV_HEADS, s // LANES, LANES, HEAD_DIM)
            mixT = _window_attention(qT, k5, vT5, p["b_sink"][j], GQA_GROUP, tq_full)
        else:
            gq = jnp.broadcast_to(p["c_qk_norm"][j][0][:, None], (HEAD_DIM, LANES))
            gk = jnp.tile(p["c_qk_norm"][j][1], LANES // HEAD_DIM)[None, :]
            qT, k4, vT5, qmT = _inproj(x2d, p["c_wqT"][j], p["c_wvT"][j], p["c_wmT"][j], p["c_wk"][j],
                                       tabs2T, tabs2, gq, gk, bd, half=HEAD_DIM // 4,
                                       qk_norm=True, vchunk=tk_full, **common)
            tk = min(tk_full, s)
            k5 = k4.reshape(b, GQA_KV_HEADS, s // tk, tk, HEAD_DIM)
            mixT = _gqa_attention(qT, k5, vT5, GQA_GROUP, tq_full)

        km, vmT = _memproj(mem2d, p["w_memk"][i], p["w_memvT"][i], b, m_len)
        memT = _gqa_attention(qmT, km.reshape(b, MEM_HEADS, 1, m_len, HEAD_DIM), vmT, 1, 2 * tq_full)

        x2d, affT = _oproj(mixT, memT, x2d, p["w_o"][i], p["ln_mix"][i], p["w_routerT"][i], alpha, b, s)

        affT3 = affT.reshape(N_EXPERTS, n // LANES, LANES)
        posm, off = _route_select(affT3, cap)
        idx, gates = _route_compact(off, posm, affT3, cap)
        y = _expert_ffn(idx, x2d, p["w_gate_up"][i], p["w_down"][i], gates)
        x2d = _combine_ln(off, x2d, posm, y, p["ln_ffn"][i], alpha, cap)
    return x2d.reshape(b, s, d)


def kernel(x_prompt, x_sample, mem_prompt, mem_sample, a_w_in, a_lambda, a_subln, b_w_in, b_sink, c_w_in, c_qk_norm, w_mem_kv, w_o, ln_mix, w_router, w_gate_up, w_down, ln_ffn):
    p = _prep_params(dict(
        a_w_in=a_w_in, a_lambda=a_lambda, a_subln=a_subln, b_w_in=b_w_in, b_sink=b_sink,
        c_w_in=c_w_in, c_qk_norm=c_qk_norm, w_mem_kv=w_mem_kv, w_o=w_o, ln_mix=ln_mix,
        w_router=w_router, w_gate_up=w_gate_up, w_down=w_down, ln_ffn=ln_ffn))
    return (_encode(x_prompt, mem_prompt, p), _encode(x_sample, mem_sample, p))
```

```python
import functools
import math

import jax
import jax.numpy as jnp
from jax import lax
from jax.experimental import pallas as pl
from jax.experimental.pallas import tpu as pltpu

F32 = jnp.float32
BF16 = jnp.bfloat16

HEAD_DIM = 64
GQA_KV_HEADS = 4
GQA_GROUP = 3
A_HEADS = 6
A_V_DIM = 2 * HEAD_DIM
MEM_HEADS = 4
MIX_W = 768
MEM_W = MEM_HEADS * HEAD_DIM
KV_W = GQA_KV_HEADS * HEAD_DIM
N_MIXERS = 3
WINDOW = 128
GRID_W = 64
ROPE_THETA = 10000.0
N_EXPERTS = 16
EC_FACTOR = 2
LN_EPS = 1e-5
RMS_EPS = 1e-6
LOG2E = 1.4426950408889634
QK_SCALE = HEAD_DIM ** -0.5 * LOG2E
LANES = 128
ONES_ROWS = 16
VMEM_LIMIT = 52 * 1024 * 1024

_NT = (((1,), (1,)), ((), ()))
_TN = (((0,), (0,)), ((), ()))


def _cparams(grid_rank):
    return pltpu.CompilerParams(dimension_semantics=("arbitrary",) * grid_rank,
                                vmem_limit_bytes=VMEM_LIMIT)


def _layer_norm(h, g, b):
    mu = jnp.mean(h, axis=-1, keepdims=True)
    d = h - mu
    var = jnp.mean(d * d, axis=-1, keepdims=True)
    return d * lax.rsqrt(var + LN_EPS) * g + b


def _lane_tile(a, width):
    return jnp.concatenate([a] * (width // a.shape[1]), axis=1)


def _inproj_kernel(x_ref, wqT_ref, wvT_ref, wmT_ref, wk_ref, cosT_ref, sinT_ref, cos_ref, sin_ref,
                   gT_ref, g_ref, bd_ref, qT_ref, k_ref, vT_ref, qmT_ref, *, half, qk_norm, vchunk):
    xb = x_ref[...].astype(BF16)
    tm = xb.shape[0]
    cosT = cosT_ref[...]
    sinT = sinT_ref[...]
    nparts = HEAD_DIM // half

    yqT = lax.dot_general(wqT_ref[...], xb, _NT, preferred_element_type=F32)
    for h in range(yqT.shape[0] // HEAD_DIM):
        xh = yqT[h * HEAD_DIM:(h + 1) * HEAD_DIM]
        if qk_norm:
            ms = jnp.mean(xh * xh, axis=0, keepdims=True)
            xh = xh * lax.rsqrt(ms + RMS_EPS) * _lane_tile(gT_ref[...], tm)
        rot = jnp.concatenate([xh[(p ^ 1) * half:((p ^ 1) + 1) * half] for p in range(nparts)], axis=0)
        qT_ref[0, h * HEAD_DIM:(h + 1) * HEAD_DIM, :] = ((xh * cosT + rot * sinT) * QK_SCALE).astype(BF16)

    yvT = lax.dot_general(wvT_ref[...], xb, _NT, preferred_element_type=F32).astype(BF16)
    for c in range(tm // vchunk):
        vT_ref[0, c] = yvT[:, c * vchunk:(c + 1) * vchunk]
    ymT = lax.dot_general(wmT_ref[...], xb, _NT, preferred_element_type=F32)
    qmT_ref[0] = (ymT * QK_SCALE).astype(BF16)

    lane = lax.broadcasted_iota(jnp.int32, (tm, LANES), 1)
    first = (lane % (2 * half)) < half
    cos = cos_ref[...]
    sin = sin_ref[...]
    yk = jnp.dot(xb, wk_ref[...], preferred_element_type=F32)
    for c in range(yk.shape[1] // LANES):
        ch = yk[:, c * LANES:(c + 1) * LANES]
        if qk_norm:
            sq = ch * ch
            hi = sq.astype(BF16)
            lo = (sq - hi.astype(F32)).astype(BF16)
            ss = (jnp.dot(hi, bd_ref[...], preferred_element_type=F32)
                  + jnp.dot(lo, bd_ref[...], preferred_element_type=F32))
            ch = ch * lax.rsqrt(ss * (1.0 / HEAD_DIM) + RMS_EPS) * g_ref[...]
        rot = jnp.where(first, pltpu.roll(ch, LANES - half, 1), pltpu.roll(ch, half, 1))
        ch = (ch * cos + rot * sin).astype(BF16)
        k_ref[0, 2 * c] = ch[:, 0:HEAD_DIM]
        k_ref[0, 2 * c + 1] = ch[:, HEAD_DIM:LANES]


def _inproj(x2d, wqT, wvT, wmT, wk, tabsT, tabs, gT, g, bd, *, batch, seq, half, qk_norm, vchunk):
    n, d = x2d.shape
    tm = min(512, seq)
    vchunk = min(vchunk, tm)
    nb = seq // tm
    qw, vw, kw = wqT.shape[0], wvT.shape[0], wk.shape[1]
    kern = functools.partial(_inproj_kernel, half=half, qk_norm=qk_norm, vchunk=vchunk)
    const = lambda b, i: (0, 0)
    return pl.pallas_call(
        kern,
        grid=(batch, nb),
        in_specs=[
            pl.BlockSpec((tm, d), lambda b, i: (b * nb + i, 0)),
            pl.BlockSpec(wqT.shape, const),
            pl.BlockSpec(wvT.shape, const),
            pl.BlockSpec(wmT.shape, const),
            pl.BlockSpec(wk.shape, const),
            pl.BlockSpec((HEAD_DIM, tm), lambda b, i: (0, i)),
            pl.BlockSpec((HEAD_DIM, tm), lambda b, i: (0, i)),
            pl.BlockSpec((tm, LANES), lambda b, i: (i, 0)),
            pl.BlockSpec((tm, LANES), lambda b, i: (i, 0)),
            pl.BlockSpec((HEAD_DIM, LANES), const),
            pl.BlockSpec((1, LANES), const),
            pl.BlockSpec((LANES, LANES), const),
        ],
        out_specs=[
            pl.BlockSpec((1, qw, tm), lambda b, i: (b, 0, i)),
            pl.BlockSpec((1, kw // HEAD_DIM, tm, HEAD_DIM), lambda b, i: (b, 0, i, 0)),
            pl.BlockSpec((1, tm // vchunk, vw, vchunk), lambda b, i: (b, i, 0, 0)),
            pl.BlockSpec((1, MEM_W, tm), lambda b, i: (b, 0, i)),
        ],
        out_shape=[
            jax.ShapeDtypeStruct((batch, qw, seq), BF16),
            jax.ShapeDtypeStruct((batch, kw // HEAD_DIM, seq, HEAD_DIM), BF16),
            jax.ShapeDtypeStruct((batch, seq // vchunk, vw, vchunk), BF16),
            jax.ShapeDtypeStruct((batch, MEM_W, seq), BF16),
        ],
        compiler_params=_cparams(2),
        name="inproj",
    )(x2d, wqT, wvT, wmT, wk, *tabsT, *tabs, gT, g, bd)


def _memproj_kernel(x_ref, wk_ref, wvT_ref, k_ref, vT_ref):
    xb = x_ref[...].astype(BF16)
    yk = jnp.dot(xb, wk_ref[...], preferred_element_type=F32).astype(BF16)
    for h in range(MEM_HEADS):
        k_ref[0, h] = yk[:, h * HEAD_DIM:(h + 1) * HEAD_DIM]
    vT_ref[0, 0] = lax.dot_general(wvT_ref[...], xb, _NT, preferred_element_type=F32).astype(BF16)


def _memproj(mem2d, wk, wvT, batch, m_len):
    d = mem2d.shape[1]
    return pl.pallas_call(
        _memproj_kernel,
        grid=(batch,),
        in_specs=[pl.BlockSpec((m_len, d), lambda b: (b, 0)),
                  pl.BlockSpec(wk.shape, lambda b: (0, 0)),
                  pl.BlockSpec(wvT.shape, lambda b: (0, 0))],
        out_specs=[pl.BlockSpec((1, MEM_HEADS, m_len, HEAD_DIM), lambda b: (b, 0, 0, 0)),
                   pl.BlockSpec((1, 1, MEM_W, m_len), lambda b: (b, 0, 0, 0))],
        out_shape=[jax.ShapeDtypeStruct((batch, MEM_HEADS, m_len, HEAD_DIM), BF16),
                   jax.ShapeDtypeStruct((batch, 1, MEM_W, m_len), BF16)],
        compiler_params=_cparams(1),
        name="memproj",
    )(mem2d, wk, wvT)


def _ones_rows(tk):
    return (lax.broadcasted_iota(jnp.int32, (ONES_ROWS, tk), 0) == 0).astype(BF16)


def _flash_step(sT, va, m, acc):
    m_new = jnp.maximum(m, jnp.max(sT, axis=0, keepdims=True))
    a = jnp.exp2(m - m_new)
    pT = jnp.exp2(sT - m_new).astype(BF16)
    return m_new, a * acc + jnp.dot(va, pT, preferred_element_type=F32)


def _gqa_kernel(qT_ref, k_ref, vT_ref, o_ref, *, group, nk, unroll):
    tk = k_ref.shape[3]
    qT = jnp.concatenate([qT_ref[0, g * HEAD_DIM:(g + 1) * HEAD_DIM, :] for g in range(group)], axis=1)
    nq = qT.shape[1]
    tq = nq // group
    ones = _ones_rows(tk)

    def body(j, carry):
        sT = jnp.dot(k_ref[0, 0, j], qT, preferred_element_type=F32)
        va = jnp.concatenate([vT_ref[0, j], ones], axis=0)
        return _flash_step(sT, va, *carry)

    init = (jnp.full((1, nq), -jnp.inf, F32), jnp.zeros((HEAD_DIM + ONES_ROWS, nq), F32))
    _, acc = lax.fori_loop(0, nk, body, init, unroll=unroll)
    o = acc[0:HEAD_DIM] / acc[HEAD_DIM:HEAD_DIM + 1]
    for g in range(group):
        o_ref[0, g * HEAD_DIM:(g + 1) * HEAD_DIM, :] = o[:, g * tq:(g + 1) * tq].astype(o_ref.dtype)


def _gqa_attention(qT, k5, vT5, group, tq):
    b, qrows, s = qT.shape
    _, kvh, nk, tk, _ = k5.shape
    tq = min(tq, s)
    kern = functools.partial(_gqa_kernel, group=group, nk=nk, unroll=math.gcd(nk, 4))
    return pl.pallas_call(
        kern,
        grid=(b, kvh, s // tq),
        in_specs=[
            pl.BlockSpec((1, group * HEAD_DIM, tq), lambda bi, h, i: (bi, h, i)),
            pl.BlockSpec((1, 1, nk, tk, HEAD_DIM), lambda bi, h, i: (bi, h, 0, 0, 0)),
            pl.BlockSpec((1, nk, HEAD_DIM, tk), lambda bi, h, i: (bi, 0, h, 0)),
        ],
        out_specs=pl.BlockSpec((1, group * HEAD_DIM, tq), lambda bi, h, i: (bi, h, i)),
        out_shape=jax.ShapeDtypeStruct(qT.shape, BF16),
        compiler_params=_cparams(3),
        name="gqa_attention",
    )(qT, k5, vT5)


def _diff_kernel(lam_ref, g_ref, qT_ref, k_ref, vT_ref, o_ref, *, nk, unroll, lam_init):
    tk = k_ref.shape[3]
    tq = qT_ref.shape[2]
    lp = lam_ref[...]
    lam = (jnp.exp(jnp.sum(lp[0:1] * lp[1:2], axis=-1, keepdims=True))
           - jnp.exp(jnp.sum(lp[2:3] * lp[3:4], axis=-1, keepdims=True)) + lam_init)
    q0T = qT_ref[0, 0:HEAD_DIM, :]
    q1T = qT_ref[0, HEAD_DIM:2 * HEAD_DIM, :]
    ones = _ones_rows(tk)

    def body(j, carry):
        sT = jnp.concatenate([jnp.dot(k_ref[0, 0, j], q0T, preferred_element_type=F32),
                              jnp.dot(k_ref[0, 1, j], q1T, preferred_element_type=F32)], axis=1)
        va = jnp.concatenate([vT_ref[0, j], ones], axis=0)
        return _flash_step(sT, va, *carry)

    init = (jnp.full((1, 2 * tq), -jnp.inf, F32), jnp.zeros((A_V_DIM + ONES_ROWS, 2 * tq), F32))
    _, acc = lax.fori_loop(0, nk, body, init, unroll=unroll)
    o = acc[0:A_V_DIM] / acc[A_V_DIM:A_V_DIM + 1]
    o = o[:, 0:tq] - lam * o[:, tq:2 * tq]
    ms = jnp.mean(o * o, axis=0, keepdims=True)
    o = o * lax.rsqrt(ms + RMS_EPS) * _lane_tile(g_ref[...], tq) * (1.0 - lam_init)
    o_ref[0] = o.astype(o_ref.dtype)


def _diff_attention(qT, k5, vT5, lam_p, g_col, lam_init, tq):
    b, qrows, s = qT.shape
    _, _, nk, tk, _ = k5.shape
    tq = min(tq, s)
    kern = functools.partial(_diff_kernel, nk=nk, unroll=math.gcd(nk, 4), lam_init=lam_init)
    return pl.pallas_call(
        kern,
        grid=(b, A_HEADS, s // tq),
        in_specs=[
            pl.BlockSpec((4, HEAD_DIM), lambda bi, h, i: (0, 0)),
            pl.BlockSpec((A_V_DIM, LANES), lambda bi, h, i: (0, 0)),
            pl.BlockSpec((1, 2 * HEAD_DIM, tq), lambda bi, h, i: (bi, h, i)),
            pl.BlockSpec((1, 2, nk, tk, HEAD_DIM), lambda bi, h, i: (bi, h, 0, 0, 0)),
            pl.BlockSpec((1, nk, A_V_DIM, tk), lambda bi, h, i: (bi, 0, h, 0)),
        ],
        out_specs=pl.BlockSpec((1, A_V_DIM, tq), lambda bi, h, i: (bi, h, i)),
        out_shape=jax.ShapeDtypeStruct(qT.shape, BF16),
        compiler_params=_cparams(3),
        name="diff_attention",
    )(lam_p, g_col, qT, k5, vT5)


def _window_kernel(sink_ref, qT_ref, k_ref, vT_ref, o_ref, *, group, nwc, nchunks):
    h = pl.program_id(1)
    tq = qT_ref.shape[2]
    nq = group * tq
    q0 = pl.program_id(2) * tq
    c0 = jnp.clip(q0 // LANES - WINDOW // LANES, 0, nchunks - nwc)
    qT = jnp.concatenate([qT_ref[0, g * HEAD_DIM:(g + 1) * HEAD_DIM, :] for g in range(group)], axis=1)
    qpos = q0 + lax.broadcasted_iota(jnp.int32, (LANES, nq), 1) % tq
    krow = lax.broadcasted_iota(jnp.int32, (LANES, nq), 0)
    sink = jnp.concatenate(
        [jnp.full((1, tq), sink_ref[h * group + g] * LOG2E, F32) for g in range(group)], axis=1)
    ones = _ones_rows(LANES)
    scores = []
    m = sink
    for c in range(nwc):
        sT = jnp.dot(k_ref[0, 0, c0 + c], qT, preferred_element_type=F32)
        kpos = (c0 + c) * LANES + krow
        sT = jnp.where(jnp.abs(kpos - qpos) <= WINDOW, sT, -jnp.inf)
        scores.append(sT)
        m = jnp.maximum(m, jnp.max(sT, axis=0, keepdims=True))
    acc = jnp.zeros((HEAD_DIM + ONES_ROWS, nq), F32)
    for c in range(nwc):
        va = jnp.concatenate([vT_ref[0, c0 + c], ones], axis=0)
        acc = acc + jnp.dot(va, jnp.exp2(scores[c] - m).astype(BF16), preferred_element_type=F32)
    o = acc[0:HEAD_DIM] / (acc[HEAD_DIM:HEAD_DIM + 1] + jnp.exp2(sink - m))
    for g in range(group):
        o_ref[0, g * HEAD_DIM:(g + 1) * HEAD_DIM, :] = o[:, g * tq:(g + 1) * tq].astype(o_ref.dtype)


def _window_attention(qT, k5, vT5, sink, group, tq):
    b, qrows, s = qT.shape
    _, kvh, nchunks, tk, _ = k5.shape
    tq = min(tq, s)
    nwc = min(nchunks, tq // LANES + 2 * (WINDOW // LANES))
    kern = functools.partial(_window_kernel, group=group, nwc=nwc, nchunks=nchunks)
    return pl.pallas_call(
        kern,
        grid=(b, kvh, s // tq),
        in_specs=[
            pl.BlockSpec(memory_space=pltpu.SMEM),
            pl.BlockSpec((1, group * HEAD_DIM, tq), lambda bi, h, i: (bi, h, i)),
            pl.BlockSpec((1, 1, nchunks, tk, HEAD_DIM), lambda bi, h, i: (bi, h, 0, 0, 0)),
            pl.BlockSpec((1, nchunks, HEAD_DIM, tk), lambda bi, h, i: (bi, 0, h, 0)),
        ],
        out_specs=pl.BlockSpec((1, group * HEAD_DIM, tq), lambda bi, h, i: (bi, h, i)),
        out_shape=jax.ShapeDtypeStruct(qT.shape, BF16),
        compiler_params=_cparams(3),
        name="window_attention",
    )(sink, qT, k5, vT5)


def _oproj_kernel(mixT_ref, memT_ref, x_ref, wo_ref, ln_ref, wrT_ref, xo_ref, affT_ref, *, alpha):
    sub = (lax.dot_general(mixT_ref[0], wo_ref[0:MIX_W, :], _TN, preferred_element_type=F32)
           + lax.dot_general(memT_ref[0], wo_ref[MIX_W:MIX_W + MEM_W, :], _TN,
                             preferred_element_type=F32))
    y = _layer_norm(alpha * x_ref[...] + sub, ln_ref[0:1, :], ln_ref[1:2, :])
    xo_ref[...] = y
    logitsT = lax.dot_general(wrT_ref[...], y.astype(BF16), _NT, preferred_element_type=F32)
    e = jnp.exp(logitsT - jnp.max(logitsT, axis=0, keepdims=True))
    affT_ref[...] = e / jnp.sum(e, axis=0, keepdims=True)


def _oproj(mixT, memT, x2d, wo, ln, wrT, alpha, batch, seq):
    n, d = x2d.shape
    tm = min(512, seq)
    nb = seq // tm
    kern = functools.partial(_oproj_kernel, alpha=alpha)
    const = lambda b, i: (0, 0)
    row = lambda b, i: (b * nb + i, 0)
    return pl.pallas_call(
        kern,
        grid=(batch, nb),
        in_specs=[
            pl.BlockSpec((1, MIX_W, tm), lambda b, i: (b, 0, i)),
            pl.BlockSpec((1, MEM_W, tm), lambda b, i: (b, 0, i)),
            pl.BlockSpec((tm, d), row),
            pl.BlockSpec(wo.shape, const),
            pl.BlockSpec((2, d), const),
            pl.BlockSpec(wrT.shape, const),
        ],
        out_specs=[
            pl.BlockSpec((tm, d), row),
            pl.BlockSpec((N_EXPERTS, tm), lambda b, i: (0, b * nb + i)),
        ],
        out_shape=[
            jax.ShapeDtypeStruct((n, d), F32),
            jax.ShapeDtypeStruct((N_EXPERTS, n), F32),
        ],
        compiler_params=_cparams(2),
        name="oproj_ln_router",
    )(mixT, memT, x2d, wo, ln, wrT)


def _route_select_kernel(affT_ref, tri_ref, ones_ref, low_ref, posm_ref, off_ref, *, cap):
    a = affT_ref[0]
    bits = lax.bitcast_convert_type(a, jnp.int32)

    def count(mask):
        return jnp.sum(jnp.sum(mask.astype(F32), axis=1, keepdims=True), axis=0, keepdims=True)

    def bisect(i, t):
        cand = t | jnp.left_shift(jnp.int32(1), 30 - i)
        return jnp.where(count(bits >= cand) >= cap, cand, t)

    t = lax.fori_loop(0, 31, bisect, jnp.zeros((1, 1), jnp.int32))

    def excl_cumsum(mask):
        mb = mask.astype(BF16)
        incl = jnp.dot(mb, tri_ref[...], preferred_element_type=F32)
        tot = jnp.dot(mb, ones_ref[...], preferred_element_type=F32)
        before = jnp.dot(low_ref[...], tot.astype(BF16), preferred_element_type=F32)
        return before + incl - mask.astype(F32), before

    gt = bits > t
    eq = bits == t
    need = cap - count(gt)
    rank, _ = excl_cumsum(eq)
    sel = gt | (eq & (rank < need))
    pos, before = excl_cumsum(sel)
    posm_ref[0] = jnp.where(sel, pos, -1.0).astype(jnp.int32)
    off_ref[0] = before[:, 0:1].astype(jnp.int32)


def _route_select(affT3, cap):
    e, r, _ = affT3.shape
    tri = (jnp.arange(LANES)[:, None] <= jnp.arange(LANES)[None, :]).astype(BF16)
    ones = jnp.ones((LANES, LANES), BF16)
    low = (jnp.arange(r)[None, :] < jnp.arange(r)[:, None]).astype(BF16)
    posm, off = pl.pallas_call(
        functools.partial(_route_select_kernel, cap=cap),
        grid=(e,),
        in_specs=[pl.BlockSpec((1, r, LANES), lambda ei: (ei, 0, 0)),
                  pl.BlockSpec((LANES, LANES), lambda ei: (0, 0)),
                  pl.BlockSpec((LANES, LANES), lambda ei: (0, 0)),
                  pl.BlockSpec((r, r), lambda ei: (0, 0))],
        out_specs=[pl.BlockSpec((1, r, LANES), lambda ei: (ei, 0, 0)),
                   pl.BlockSpec((1, r, 1), lambda ei: (ei, 0, 0))],
        out_shape=[jax.ShapeDtypeStruct((e, r, LANES), jnp.int32),
                   jax.ShapeDtypeStruct((e, r, 1), jnp.int32)],
        compiler_params=_cparams(1),
        name="route_select",
    )(affT3, tri, ones, low)
    return posm, off.reshape(e, r)


COMPACT_ROWS = 16


def _route_compact_kernel(off_ref, posm_ref, affT_ref, idx_ref, gate_ref, acc_ref, *, nrows, ncb):
    e = pl.program_id(0)
    acc_ref[...] = jnp.zeros_like(acc_ref)
    slot = lax.broadcasted_iota(jnp.int32, (LANES, LANES), 0)
    sub = lax.broadcasted_iota(jnp.int32, (COMPACT_ROWS, LANES), 0)
    lane = lax.broadcasted_iota(jnp.int32, (COMPACT_ROWS, LANES), 1).astype(F32)

    def body(r, carry):
        cb = lax.shift_right_logical(off_ref[e, r], 7)
        rel = posm_ref[0, pl.ds(r, 1), :] - cb * LANES
        a = affT_ref[0, pl.ds(r, 1), :]
        hi = a.astype(BF16).astype(F32)
        mid = (a - hi).astype(BF16).astype(F32)
        lo = a - hi - mid
        vals = jnp.where(sub == 0, lane, jnp.where(sub == 1, jnp.asarray(r, F32), jnp.where(
            sub == 2, hi, jnp.where(sub == 3, mid, jnp.where(sub == 4, lo, 0.0))))).astype(BF16)
        for part in range(2):
            hit = ((rel - part * LANES) == slot).astype(BF16)
            acc_ref[cb + part] += lax.dot_general(vals, hit, _NT, preferred_element_type=F32)
        return carry

    lax.fori_loop(0, nrows, body, 0, unroll=8)
    acc = acc_ref[0:ncb]
    idx_ref[0] = (acc[:, 1:2, :] * LANES + acc[:, 0:1, :]).astype(jnp.int32)
    gate_ref[0] = acc[:, 2:3, :] + acc[:, 3:4, :] + acc[:, 4:5, :]


def _route_compact(off, posm, affT3, cap):
    e, r, _ = posm.shape
    ncb = cap // LANES
    idx, gate = pl.pallas_call(
        functools.partial(_route_compact_kernel, nrows=r, ncb=ncb),
        grid=(e,),
        in_specs=[pl.BlockSpec(memory_space=pltpu.SMEM),
                  pl.BlockSpec((1, r, LANES), lambda ei: (ei, 0, 0)),
                  pl.BlockSpec((1, r, LANES), lambda ei: (ei, 0, 0))],
        out_specs=[pl.BlockSpec((1, ncb, 1, LANES), lambda ei: (ei, 0, 0, 0)),
                   pl.BlockSpec((1, ncb, 1, LANES), lambda ei: (ei, 0, 0, 0))],
        out_shape=[jax.ShapeDtypeStruct((e, ncb, 1, LANES), jnp.int32),
                   jax.ShapeDtypeStruct((e, ncb, 1, LANES), F32)],
        scratch_shapes=[pltpu.VMEM((ncb + 2, COMPACT_ROWS, LANES), F32)],
        compiler_params=_cparams(1),
        name="route_compact",
    )(off, posm, affT3)
    return idx.reshape(e, cap), gate.reshape(e, cap, 1)


FFN_ROWS = 512


def _ffn_kernel(idx_ref, nxt_ref, x_hbm, wg_ref, wu_ref, wd_ref, gate_ref, y_ref,
                xf_ref, xb_ref, acc_ref, sem, *, nf, nblk):
    f = pl.program_id(2)
    blk = pl.program_id(0) * pl.num_programs(1) + pl.program_id(1)
    slot = blk % 2
    tc = xb_ref.shape[0]
    tcp = xf_ref.shape[1]
    per_step = tcp // nf
    chunk = min(FFN_ROWS, tc)

    def row_copy(ids, i, s):
        return pltpu.make_async_copy(x_hbm.at[pl.ds(ids[0, 0, i], 1)], xf_ref.at[s, pl.ds(i, 1)], sem.at[s])

    def all_rows(ids, s, fn):
        def one(i, c):
            fn(row_copy(ids, i, s))
            return c
        lax.fori_loop(0, tcp, one, 0, unroll=8)

    @pl.when(f == 0)
    def _():
        @pl.when(blk == 0)
        def _():
            all_rows(idx_ref, slot, lambda c: c.start())

        all_rows(idx_ref, slot, lambda c: c.wait())
        for c in range(tc // chunk):
            rows = slice(c * chunk, (c + 1) * chunk)
            xb_ref[rows, :] = xf_ref[slot, rows, :].astype(BF16)
        acc_ref[...] = jnp.zeros_like(acc_ref)

    wg = wg_ref[0, 0].astype(BF16)
    wu = wu_ref[0, 0].astype(BF16)
    wd = wd_ref[0, 0].astype(BF16)
    for c in range(tc // chunk):
        rows = slice(c * chunk, (c + 1) * chunk)
        x = xb_ref[rows, :]
        g = jnp.dot(x, wg, preferred_element_type=F32)
        u = jnp.dot(x, wu, preferred_element_type=F32)
        h = (g / (1.0 + jnp.exp(-g))) * u
        acc_ref[rows, :] += jnp.dot(h.astype(BF16), wd, preferred_element_type=F32)

    for j in range(per_step):
        row_copy(nxt_ref, f * per_step + j, 1 - slot).start()

    @pl.when(f == nf - 1)
    def _():
        y_ref[0] = (acc_ref[...] * gate_ref[0]).astype(y_ref.dtype)

        @pl.when(blk == nblk - 1)
        def _():
            all_rows(nxt_ref, 1 - slot, lambda c: c.wait())


def _expert_ffn(idx, x2d, w_gu, w_dn, layer, gates):
    e, c = idx.shape
    d = x2d.shape[1]
    fdim = w_dn.shape[2]
    tc = min(2048, c)
    ncb = c // tc
    nblk = e * ncb
    tf = 256
    nf = fdim // tf
    assert tc % min(FFN_ROWS, tc) == 0
    per_step = -(-tc // nf)
    per_step += -per_step % 8
    tcp = per_step * nf
    kern = functools.partial(_ffn_kernel, nf=nf, nblk=nblk)
    idx3 = jnp.pad(idx.reshape(nblk, 1, tc), ((0, 0), (0, 0), (0, tcp - tc)))
    return pl.pallas_call(
        kern,
        grid=(e, ncb, nf),
        in_specs=[
            pl.BlockSpec((1, 1, tcp), lambda ei, ci, fi: (ei * ncb + ci, 0, 0), memory_space=pltpu.SMEM),
            pl.BlockSpec((1, 1, tcp), lambda ei, ci, fi: (jnp.minimum(ei * ncb + ci + 1, nblk - 1), 0, 0),
                         memory_space=pltpu.SMEM),
            pl.BlockSpec(memory_space=pl.ANY),
            pl.BlockSpec((1, 1, d, tf), lambda ei, ci, fi: (layer, ei, 0, fi)),
            pl.BlockSpec((1, 1, d, tf), lambda ei, ci, fi: (layer, ei, 0, fi + nf)),
            pl.BlockSpec((1, 1, tf, d), lambda ei, ci, fi: (layer, ei, fi, 0)),
            pl.BlockSpec((1, tc, 1), lambda ei, ci, fi: (ei, ci, 0)),
        ],
        out_specs=pl.BlockSpec((1, tc, d), lambda ei, ci, fi: (ei, ci, 0)),
        out_shape=jax.ShapeDtypeStruct((e, c, d), BF16),
        scratch_shapes=[pltpu.VMEM((2, tcp, d), F32), pltpu.VMEM((tc, d), BF16), pltpu.VMEM((tc, d), F32),
                        pltpu.SemaphoreType.DMA((2,))],
        compiler_params=_cparams(3),
        name="expert_ffn",
    )(idx3, idx3, x2d, w_gu, w_gu, w_dn, gates)


COMBINE_ROWS = 8
BF16_ROWS = 16
WIN = LANES + BF16_ROWS
WIN_SHORT = 3 * BF16_ROWS


def _combine_kernel(off_ref, x_ref, posm_ref, y_hbm, ln_ref, o_ref, ybuf, sem, *, alpha, cap):
    t = pl.program_id(0)
    n_exp = posm_ref.shape[0]
    d = x_ref.shape[1]
    slot = lax.broadcasted_iota(jnp.int32, (WIN, LANES), 0)

    @pl.when(t == 0)
    def _():
        ybuf[...] = jnp.zeros_like(ybuf)

    def win_start(rr, e):
        start = lax.shift_right_logical(off_ref[e, t * COMBINE_ROWS + rr], 4) * BF16_ROWS
        return pl.multiple_of(jnp.minimum(start, cap - WIN), BF16_ROWS)

    def for_each_window(rr, par, fn):
        for e in range(n_exp):
            start = win_start(rr, e)
            short = off_ref[e, t * COMBINE_ROWS + rr + 1] - start <= WIN_SHORT
            for cond, rows in ((short, WIN_SHORT), (jnp.logical_not(short), WIN)):
                @pl.when(cond)
                def _():
                    fn(pltpu.make_async_copy(y_hbm.at[e, pl.ds(start, rows)],
                                             ybuf.at[par, e, pl.ds(0, rows)], sem.at[par]))

    def row(rr, par):
        @pl.when(rr + 1 < COMBINE_ROWS)
        def _():
            for_each_window(rr + 1, 1 - par, lambda c: c.start())
        for_each_window(rr, par, lambda c: c.wait())
        hit = jnp.concatenate(
            [((posm_ref[e, pl.ds(rr, 1), :] - win_start(rr, e)) == slot).astype(BF16) for e in range(n_exp)],
            axis=0)
        tot = lax.dot_general(hit, ybuf[par].reshape(n_exp * WIN, d), _TN, preferred_element_type=F32)
        rows = pl.ds(pl.multiple_of(rr * LANES, LANES), LANES)
        o_ref[rows, :] = _layer_norm(alpha * x_ref[rows, :] + tot, ln_ref[0:1, :], ln_ref[1:2, :])

    for_each_window(0, 0, lambda c: c.start())

    def body(k, carry):
        row(2 * k, 0)
        row(2 * k + 1, 1)
        return carry

    lax.fori_loop(0, COMBINE_ROWS // 2, body, 0)


def _combine_ln(off, x2d, posm, y, ln, alpha, cap):
    n, d = x2d.shape
    e, r, _ = posm.shape
    rows = COMBINE_ROWS
    assert r % rows == 0 and rows % 2 == 0 and cap >= WIN
    tt = rows * LANES
    off = jnp.concatenate([off, jnp.full((e, 1), cap, jnp.int32)], axis=1)
    return pl.pallas_call(
        functools.partial(_combine_kernel, alpha=alpha, cap=cap),
        grid=(n // tt,),
        in_specs=[pl.BlockSpec(memory_space=pltpu.SMEM),
                  pl.BlockSpec((tt, d), lambda ti: (ti, 0)),
                  pl.BlockSpec((e, rows, LANES), lambda ti: (0, ti, 0)),
                  pl.BlockSpec(memory_space=pl.ANY),
                  pl.BlockSpec((2, d), lambda ti: (0, 0))],
        out_specs=pl.BlockSpec((tt, d), lambda ti: (ti, 0)),
        out_shape=jax.ShapeDtypeStruct((n, d), F32),
        scratch_shapes=[pltpu.VMEM((2, e, WIN, d), BF16), pltpu.SemaphoreType.DMA((2,))],
        compiler_params=_cparams(1),
        name="combine_ln",
    )(off, x2d, posm, y, ln)


def _rope_tables(seq):
    def tab(pos, dim):
        inv = ROPE_THETA ** (-jnp.arange(0, dim, 2, dtype=F32) / dim)
        ang = pos.astype(F32)[:, None] * inv[None, :]
        return jnp.cos(ang), jnp.sin(ang)

    c1, s1 = tab(jnp.arange(seq), HEAD_DIM)
    cos1 = jnp.concatenate([c1, c1], axis=-1)
    sin1 = jnp.concatenate([-s1, s1], axis=-1)
    cr, sr = tab(jnp.arange(seq) // GRID_W, HEAD_DIM // 2)
    cc, sc = tab(jnp.arange(seq) % GRID_W, HEAD_DIM // 2)
    cos2 = jnp.concatenate([cr, cr, cc, cc], axis=-1)
    sin2 = jnp.concatenate([-sr, sr, -sc, sc], axis=-1)
    rep = LANES // HEAD_DIM
    nat = lambda c, s: (jnp.tile(c, (1, rep)), jnp.tile(s, (1, rep)))
    return ((cos1.T, sin1.T), nat(cos1, sin1)), ((cos2.T, sin2.T), nat(cos2, sin2))


def _prep_params(p):
    bf = lambda w: w.astype(BF16)
    tr = lambda w: jnp.swapaxes(w, -1, -2).astype(BF16)
    a, bw, cw = p["a_w_in"], p["b_w_in"], p["c_w_in"]
    out = dict(p)
    out.update(
        a_wqT=tr(a[:, :, 0:MIX_W]), a_wk=bf(a[:, :, MIX_W:2 * MIX_W]),
        a_wvT=tr(a[:, :, 2 * MIX_W:3 * MIX_W]), a_wmT=tr(a[:, :, 3 * MIX_W:]),
        w_memk=bf(p["w_mem_kv"][:, :, 0:MEM_W]), w_memvT=tr(p["w_mem_kv"][:, :, MEM_W:]),
        w_o=bf(p["w_o"]), w_routerT=tr(p["w_router"]),
    )
    for name, w in (("b", bw), ("c", cw)):
        out[name + "_wqT"] = tr(w[:, :, 0:MIX_W])
        out[name + "_wk"] = bf(w[:, :, MIX_W:MIX_W + KV_W])
        out[name + "_wvT"] = tr(w[:, :, MIX_W + KV_W:MIX_W + 2 * KV_W])
        out[name + "_wmT"] = tr(w[:, :, MIX_W + 2 * KV_W:])
    return out


def _encode(x, mem, p):
    b, s, d = x.shape
    n = b * s
    m_len = mem.shape[1]
    depth = p["w_o"].shape[0]
    alpha = (2 * depth) ** 0.25
    (tabs1T, tabs1), (tabs2T, tabs2) = _rope_tables(s)
    bd = (jnp.arange(LANES)[:, None] // HEAD_DIM == jnp.arange(LANES)[None, :] // HEAD_DIM).astype(BF16)
    ones_g = jnp.ones((1, LANES), F32)
    ones_gT = jnp.ones((HEAD_DIM, LANES), F32)
    cap = EC_FACTOR * n // N_EXPERTS
    x2d = x.reshape(n, d)
    mem2d = mem.reshape(b * m_len, d)
    tq_full, tk_full = 512, 512

    for i in range(depth):
        kind, j = i % N_MIXERS, i // N_MIXERS
        common = dict(batch=b, seq=s)
        if kind == 0:
            qT, k4, vT5, qmT = _inproj(x2d, p["a_wqT"][j], p["a_wvT"][j], p["a_wmT"][j], p["a_wk"][j],
                                       tabs1T, tabs1, ones_gT, ones_g, bd, half=HEAD_DIM // 2,
                                       qk_norm=False, vchunk=tk_full, **common)
            lam_init = 0.8 - 0.6 * math.exp(-0.3 * i)
            tk = min(tk_full, s)
            k5 = k4.reshape(b, 2 * A_HEADS, s // tk, tk, HEAD_DIM)
            g_col = jnp.broadcast_to(p["a_subln"][j][:, None], (A_V_DIM, LANES))
            mixT = _diff_attention(qT, k5, vT5, p["a_lambda"][j], g_col, lam_init, 2 * tq_full)
        elif kind == 1:
            qT, k4, vT5, qmT = _inproj(x2d, p["b_wqT"][j], p["b_wvT"][j], p["b_wmT"][j], p["b_wk"][j],
                                       tabs1T, tabs1, ones_gT, ones_g, bd, half=HEAD_DIM // 2,
                                       qk_norm=False, vchunk=LANES, **common)
            k5 = k4.reshape(b, GQA_KV_HEADS, s // LANES, LANES, HEAD_DIM)
            mixT = _window_attention(qT, k5, vT5, p["b_sink"][j], GQA_GROUP, tq_full)
        else:
            gq = jnp.broadcast_to(p["c_qk_norm"][j][0][:, None], (HEAD_DIM, LANES))
            gk = jnp.tile(p["c_qk_norm"][j][1], LANES // HEAD_DIM)[None, :]
            qT, k4, vT5, qmT = _inproj(x2d, p["c_wqT"][j], p["c_wvT"][j], p["c_wmT"][j], p["c_wk"][j],
                                       tabs2T, tabs2, gq, gk, bd, half=HEAD_DIM // 4,
                                       qk_norm=True, vchunk=tk_full, **common)
            tk = min(tk_full, s)
            k5 = k4.reshape(b, GQA_KV_HEADS, s // tk, tk, HEAD_DIM)
            mixT = _gqa_attention(qT, k5, vT5, GQA_GROUP, tq_full)

        km, vmT = _memproj(mem2d, p["w_memk"][i], p["w_memvT"][i], b, m_len)
        memT = _gqa_attention(qmT, km.reshape(b, MEM_HEADS, 1, m_len, HEAD_DIM), vmT, 1, 2 * tq_full)

        x2d, affT = _oproj(mixT, memT, x2d, p["w_o"][i], p["ln_mix"][i], p["w_routerT"][i], alpha, b, s)

        affT3 = affT.reshape(N_EXPERTS, n // LANES, LANES)
        posm, off = _route_select(affT3, cap)
        idx, gates = _route_compact(off, posm, affT3, cap)
        y = _expert_ffn(idx, x2d, p["w_gate_up"], p["w_down"], i, gates)
        x2d = _combine_ln(off, x2d, posm, y, p["ln_ffn"][i], alpha, cap)
    return x2d.reshape(b, s, d)


def kernel(x_prompt, x_sample, mem_prompt, mem_sample, a_w_in, a_lambda, a_subln, b_w_in, b_sink, c_w_in, c_qk_norm, w_mem_kv, w_o, ln_mix, w_router, w_gate_up, w_down, ln_ffn):
    p = _prep_params(dict(
        a_w_in=a_w_in, a_lambda=a_lambda, a_subln=a_subln, b_w_in=b_w_in, b_sink=b_sink,
        c_w_in=c_w_in, c_qk_norm=c_qk_norm, w_mem_kv=w_mem_kv, w_o=w_o, ln_mix=ln_mix,
        w_router=w_router, w_gate_up=w_gate_up, w_down=w_down, ln_ffn=ln_ffn))
    return (_encode(x_prompt, mem_prompt, p), _encode(x_sample, mem_sample, p))
```

```python
import functools
import math

import jax
import jax.numpy as jnp
from jax import lax
from jax.experimental import pallas as pl
from jax.experimental.pallas import tpu as pltpu

F32 = jnp.float32
BF16 = jnp.bfloat16

HEAD_DIM = 64
GQA_KV_HEADS = 4
GQA_GROUP = 3
A_HEADS = 6
A_V_DIM = 2 * HEAD_DIM
MEM_HEADS = 4
MIX_W = 768
MEM_W = MEM_HEADS * HEAD_DIM
KV_W = GQA_KV_HEADS * HEAD_DIM
N_MIXERS = 3
WINDOW = 128
GRID_W = 64
ROPE_THETA = 10000.0
N_EXPERTS = 16
EC_FACTOR = 2
LN_EPS = 1e-5
RMS_EPS = 1e-6
LOG2E = 1.4426950408889634
QK_SCALE = HEAD_DIM ** -0.5 * LOG2E
LANES = 128
ONES_ROWS = 16
ATTN_UNROLL = 4
VMEM_LIMIT = 52 * 1024 * 1024

_NT = (((1,), (1,)), ((), ()))
_TN = (((0,), (0,)), ((), ()))


def _cparams(grid_rank):
    return pltpu.CompilerParams(dimension_semantics=("arbitrary",) * grid_rank,
                                vmem_limit_bytes=VMEM_LIMIT)


def _layer_norm(h, g, b):
    mu = jnp.mean(h, axis=-1, keepdims=True)
    d = h - mu
    var = jnp.mean(d * d, axis=-1, keepdims=True)
    return d * lax.rsqrt(var + LN_EPS) * g + b


def _lane_tile(a, width):
    return jnp.concatenate([a] * (width // a.shape[1]), axis=1)


def _inproj_kernel(x_ref, wqT_ref, wvT_ref, wmT_ref, wk_ref, cosT_ref, sinT_ref, cos_ref, sin_ref,
                   gT_ref, g_ref, bd_ref, qT_ref, k_ref, vT_ref, qmT_ref, *, half, qk_norm, vchunk):
    xb = x_ref[...].astype(BF16)
    tm = xb.shape[0]
    cosT = cosT_ref[...]
    sinT = sinT_ref[...]
    nparts = HEAD_DIM // half

    yqT = lax.dot_general(wqT_ref[...], xb, _NT, preferred_element_type=F32)
    for h in range(yqT.shape[0] // HEAD_DIM):
        xh = yqT[h * HEAD_DIM:(h + 1) * HEAD_DIM]
        if qk_norm:
            ms = jnp.mean(xh * xh, axis=0, keepdims=True)
            xh = xh * lax.rsqrt(ms + RMS_EPS) * _lane_tile(gT_ref[...], tm)
        rot = jnp.concatenate([xh[(p ^ 1) * half:((p ^ 1) + 1) * half] for p in range(nparts)], axis=0)
        qT_ref[0, h * HEAD_DIM:(h + 1) * HEAD_DIM, :] = ((xh * cosT + rot * sinT) * QK_SCALE).astype(BF16)

    yvT = lax.dot_general(wvT_ref[...], xb, _NT, preferred_element_type=F32).astype(BF16)
    for c in range(tm // vchunk):
        vT_ref[0, c] = yvT[:, c * vchunk:(c + 1) * vchunk]
    ymT = lax.dot_general(wmT_ref[...], xb, _NT, preferred_element_type=F32)
    qmT_ref[0] = (ymT * QK_SCALE).astype(BF16)

    lane = lax.broadcasted_iota(jnp.int32, (tm, LANES), 1)
    first = (lane % (2 * half)) < half
    cos = cos_ref[...]
    sin = sin_ref[...]
    yk = jnp.dot(xb, wk_ref[...], preferred_element_type=F32)
    for c in range(yk.shape[1] // LANES):
        ch = yk[:, c * LANES:(c + 1) * LANES]
        if qk_norm:
            sq = ch * ch
            hi = sq.astype(BF16)
            lo = (sq - hi.astype(F32)).astype(BF16)
            ss = (jnp.dot(hi, bd_ref[...], preferred_element_type=F32)
                  + jnp.dot(lo, bd_ref[...], preferred_element_type=F32))
            ch = ch * lax.rsqrt(ss * (1.0 / HEAD_DIM) + RMS_EPS) * g_ref[...]
        rot = jnp.where(first, pltpu.roll(ch, LANES - half, 1), pltpu.roll(ch, half, 1))
        ch = (ch * cos + rot * sin).astype(BF16)
        k_ref[0, 2 * c] = ch[:, 0:HEAD_DIM]
        k_ref[0, 2 * c + 1] = ch[:, HEAD_DIM:LANES]


def _inproj(x2d, wqT, wvT, wmT, wk, tabsT, tabs, gT, g, bd, *, batch, seq, half, qk_norm, vchunk):
    n, d = x2d.shape
    tm = min(512, seq)
    vchunk = min(vchunk, tm)
    nb = seq // tm
    qw, vw, kw = wqT.shape[0], wvT.shape[0], wk.shape[1]
    kern = functools.partial(_inproj_kernel, half=half, qk_norm=qk_norm, vchunk=vchunk)
    const = lambda b, i: (0, 0)
    return pl.pallas_call(
        kern,
        grid=(batch, nb),
        in_specs=[
            pl.BlockSpec((tm, d), lambda b, i: (b * nb + i, 0)),
            pl.BlockSpec(wqT.shape, const),
            pl.BlockSpec(wvT.shape, const),
            pl.BlockSpec(wmT.shape, const),
            pl.BlockSpec(wk.shape, const),
            pl.BlockSpec((HEAD_DIM, tm), lambda b, i: (0, i)),
            pl.BlockSpec((HEAD_DIM, tm), lambda b, i: (0, i)),
            pl.BlockSpec((tm, LANES), lambda b, i: (i, 0)),
            pl.BlockSpec((tm, LANES), lambda b, i: (i, 0)),
            pl.BlockSpec((HEAD_DIM, LANES), const),
            pl.BlockSpec((1, LANES), const),
            pl.BlockSpec((LANES, LANES), const),
        ],
        out_specs=[
            pl.BlockSpec((1, qw, tm), lambda b, i: (b, 0, i)),
            pl.BlockSpec((1, kw // HEAD_DIM, tm, HEAD_DIM), lambda b, i: (b, 0, i, 0)),
            pl.BlockSpec((1, tm // vchunk, vw, vchunk), lambda b, i: (b, i, 0, 0)),
            pl.BlockSpec((1, MEM_W, tm), lambda b, i: (b, 0, i)),
        ],
        out_shape=[
            jax.ShapeDtypeStruct((batch, qw, seq), BF16),
            jax.ShapeDtypeStruct((batch, kw // HEAD_DIM, seq, HEAD_DIM), BF16),
            jax.ShapeDtypeStruct((batch, seq // vchunk, vw, vchunk), BF16),
            jax.ShapeDtypeStruct((batch, MEM_W, seq), BF16),
        ],
        compiler_params=_cparams(2),
        name="inproj",
    )(x2d, wqT, wvT, wmT, wk, *tabsT, *tabs, gT, g, bd)


def _memproj_kernel(x_ref, wk_ref, wvT_ref, k_ref, vT_ref):
    xb = x_ref[...].astype(BF16)
    yk = jnp.dot(xb, wk_ref[...], preferred_element_type=F32).astype(BF16)
    for h in range(MEM_HEADS):
        k_ref[0, h] = yk[:, h * HEAD_DIM:(h + 1) * HEAD_DIM]
    vT_ref[0, 0] = lax.dot_general(wvT_ref[...], xb, _NT, preferred_element_type=F32).astype(BF16)


def _memproj(mem2d, wk, wvT, batch, m_len):
    d = mem2d.shape[1]
    return pl.pallas_call(
        _memproj_kernel,
        grid=(batch,),
        in_specs=[pl.BlockSpec((m_len, d), lambda b: (b, 0)),
                  pl.BlockSpec(wk.shape, lambda b: (0, 0)),
                  pl.BlockSpec(wvT.shape, lambda b: (0, 0))],
        out_specs=[pl.BlockSpec((1, MEM_HEADS, m_len, HEAD_DIM), lambda b: (b, 0, 0, 0)),
                   pl.BlockSpec((1, 1, MEM_W, m_len), lambda b: (b, 0, 0, 0))],
        out_shape=[jax.ShapeDtypeStruct((batch, MEM_HEADS, m_len, HEAD_DIM), BF16),
                   jax.ShapeDtypeStruct((batch, 1, MEM_W, m_len), BF16)],
        compiler_params=_cparams(1),
        name="memproj",
    )(mem2d, wk, wvT)


def _ones_rows(tk):
    return (lax.broadcasted_iota(jnp.int32, (ONES_ROWS, tk), 0) == 0).astype(BF16)


def _flash_step(sT, va, m, acc):
    m_new = jnp.maximum(m, jnp.max(sT, axis=0, keepdims=True))
    a = jnp.exp2(m - m_new)
    pT = jnp.exp2(sT - m_new).astype(BF16)
    return m_new, a * acc + jnp.dot(va, pT, preferred_element_type=F32)


def _gqa_kernel(qT_ref, k_ref, vT_ref, o_ref, *, group, nk, unroll):
    tk = k_ref.shape[3]
    qT = jnp.concatenate([qT_ref[0, g * HEAD_DIM:(g + 1) * HEAD_DIM, :] for g in range(group)], axis=1)
    nq = qT.shape[1]
    tq = nq // group
    ones = _ones_rows(tk)

    def body(j, carry):
        sT = jnp.dot(k_ref[0, 0, j], qT, preferred_element_type=F32)
        va = jnp.concatenate([vT_ref[0, j], ones], axis=0)
        return _flash_step(sT, va, *carry)

    init = (jnp.full((1, nq), -jnp.inf, F32), jnp.zeros((HEAD_DIM + ONES_ROWS, nq), F32))
    _, acc = lax.fori_loop(0, nk, body, init, unroll=unroll)
    o = acc[0:HEAD_DIM] / acc[HEAD_DIM:HEAD_DIM + 1]
    for g in range(group):
        o_ref[0, g * HEAD_DIM:(g + 1) * HEAD_DIM, :] = o[:, g * tq:(g + 1) * tq].astype(o_ref.dtype)


def _gqa_attention(qT, k5, vT5, group, tq):
    b, qrows, s = qT.shape
    _, kvh, nk, tk, _ = k5.shape
    tq = min(tq, s)
    kern = functools.partial(_gqa_kernel, group=group, nk=nk, unroll=math.gcd(nk, 2 * ATTN_UNROLL))
    return pl.pallas_call(
        kern,
        grid=(b, kvh, s // tq),
        in_specs=[
            pl.BlockSpec((1, group * HEAD_DIM, tq), lambda bi, h, i: (bi, h, i)),
            pl.BlockSpec((1, 1, nk, tk, HEAD_DIM), lambda bi, h, i: (bi, h, 0, 0, 0)),
            pl.BlockSpec((1, nk, HEAD_DIM, tk), lambda bi, h, i: (bi, 0, h, 0)),
        ],
        out_specs=pl.BlockSpec((1, group * HEAD_DIM, tq), lambda bi, h, i: (bi, h, i)),
        out_shape=jax.ShapeDtypeStruct(qT.shape, BF16),
        compiler_params=_cparams(3),
        name="gqa_attention",
    )(qT, k5, vT5)


def _diff_kernel(lam_ref, g_ref, qT_ref, k_ref, vT_ref, o_ref, *, nk, unroll, lam_init):
    tk = k_ref.shape[3]
    tq = qT_ref.shape[2]
    lp = lam_ref[...]
    lam = (jnp.exp(jnp.sum(lp[0:1] * lp[1:2], axis=-1, keepdims=True))
           - jnp.exp(jnp.sum(lp[2:3] * lp[3:4], axis=-1, keepdims=True)) + lam_init)
    q0T = qT_ref[0, 0:HEAD_DIM, :]
    q1T = qT_ref[0, HEAD_DIM:2 * HEAD_DIM, :]
    ones = _ones_rows(tk)

    def body(j, carry):
        sT = jnp.concatenate([jnp.dot(k_ref[0, 0, j], q0T, preferred_element_type=F32),
                              jnp.dot(k_ref[0, 1, j], q1T, preferred_element_type=F32)], axis=1)
        va = jnp.concatenate([vT_ref[0, j], ones], axis=0)
        return _flash_step(sT, va, *carry)

    init = (jnp.full((1, 2 * tq), -jnp.inf, F32), jnp.zeros((A_V_DIM + ONES_ROWS, 2 * tq), F32))
    _, acc = lax.fori_loop(0, nk, body, init, unroll=unroll)
    o = acc[0:A_V_DIM] / acc[A_V_DIM:A_V_DIM + 1]
    o = o[:, 0:tq] - lam * o[:, tq:2 * tq]
    ms = jnp.mean(o * o, axis=0, keepdims=True)
    o = o * lax.rsqrt(ms + RMS_EPS) * _lane_tile(g_ref[...], tq) * (1.0 - lam_init)
    o_ref[0] = o.astype(o_ref.dtype)


def _diff_attention(qT, k5, vT5, lam_p, g_col, lam_init, tq):
    b, qrows, s = qT.shape
    _, _, nk, tk, _ = k5.shape
    tq = min(tq, s)
    kern = functools.partial(_diff_kernel, nk=nk, unroll=math.gcd(nk, ATTN_UNROLL), lam_init=lam_init)
    return pl.pallas_call(
        kern,
        grid=(b, A_HEADS, s // tq),
        in_specs=[
            pl.BlockSpec((4, HEAD_DIM), lambda bi, h, i: (0, 0)),
            pl.BlockSpec((A_V_DIM, LANES), lambda bi, h, i: (0, 0)),
            pl.BlockSpec((1, 2 * HEAD_DIM, tq), lambda bi, h, i: (bi, h, i)),
            pl.BlockSpec((1, 2, nk, tk, HEAD_DIM), lambda bi, h, i: (bi, h, 0, 0, 0)),
            pl.BlockSpec((1, nk, A_V_DIM, tk), lambda bi, h, i: (bi, 0, h, 0)),
        ],
        out_specs=pl.BlockSpec((1, A_V_DIM, tq), lambda bi, h, i: (bi, h, i)),
        out_shape=jax.ShapeDtypeStruct(qT.shape, BF16),
        compiler_params=_cparams(3),
        name="diff_attention",
    )(lam_p, g_col, qT, k5, vT5)


def _window_kernel(sink_ref, qT_ref, k_ref, vT_ref, o_ref, *, group, nwc, nchunks):
    h = pl.program_id(1)
    tq = qT_ref.shape[2]
    nq = group * tq
    q0 = pl.program_id(2) * tq
    c0 = jnp.clip(q0 // LANES - WINDOW // LANES, 0, nchunks - nwc)
    qT = jnp.concatenate([qT_ref[0, g * HEAD_DIM:(g + 1) * HEAD_DIM, :] for g in range(group)], axis=1)
    qpos = q0 + lax.broadcasted_iota(jnp.int32, (LANES, nq), 1) % tq
    krow = lax.broadcasted_iota(jnp.int32, (LANES, nq), 0)
    sink = jnp.concatenate(
        [jnp.full((1, tq), sink_ref[h * group + g] * LOG2E, F32) for g in range(group)], axis=1)
    ones = _ones_rows(LANES)
    scores = []
    m = sink
    for c in range(nwc):
        sT = jnp.dot(k_ref[0, 0, c0 + c], qT, preferred_element_type=F32)
        kpos = (c0 + c) * LANES + krow
        sT = jnp.where(jnp.abs(kpos - qpos) <= WINDOW, sT, -jnp.inf)
        scores.append(sT)
        m = jnp.maximum(m, jnp.max(sT, axis=0, keepdims=True))
    acc = jnp.zeros((HEAD_DIM + ONES_ROWS, nq), F32)
    for c in range(nwc):
        va = jnp.concatenate([vT_ref[0, c0 + c], ones], axis=0)
        acc = acc + jnp.dot(va, jnp.exp2(scores[c] - m).astype(BF16), preferred_element_type=F32)
    o = acc[0:HEAD_DIM] / (acc[HEAD_DIM:HEAD_DIM + 1] + jnp.exp2(sink - m))
    for g in range(group):
        o_ref[0, g * HEAD_DIM:(g + 1) * HEAD_DIM, :] = o[:, g * tq:(g + 1) * tq].astype(o_ref.dtype)


def _window_attention(qT, k5, vT5, sink, group, tq):
    b, qrows, s = qT.shape
    _, kvh, nchunks, tk, _ = k5.shape
    tq = min(tq, s)
    nwc = min(nchunks, tq // LANES + 2 * (WINDOW // LANES))
    kern = functools.partial(_window_kernel, group=group, nwc=nwc, nchunks=nchunks)
    return pl.pallas_call(
        kern,
        grid=(b, kvh, s // tq),
        in_specs=[
            pl.BlockSpec(memory_space=pltpu.SMEM),
            pl.BlockSpec((1, group * HEAD_DIM, tq), lambda bi, h, i: (bi, h, i)),
            pl.BlockSpec((1, 1, nchunks, tk, HEAD_DIM), lambda bi, h, i: (bi, h, 0, 0, 0)),
            pl.BlockSpec((1, nchunks, HEAD_DIM, tk), lambda bi, h, i: (bi, 0, h, 0)),
        ],
        out_specs=pl.BlockSpec((1, group * HEAD_DIM, tq), lambda bi, h, i: (bi, h, i)),
        out_shape=jax.ShapeDtypeStruct(qT.shape, BF16),
        compiler_params=_cparams(3),
        name="window_attention",
    )(sink, qT, k5, vT5)


def _oproj_kernel(mixT_ref, memT_ref, x_ref, wo_ref, ln_ref, wrT_ref, xo_ref, affT_ref, *, alpha):
    sub = (lax.dot_general(mixT_ref[0], wo_ref[0:MIX_W, :], _TN, preferred_element_type=F32)
           + lax.dot_general(memT_ref[0], wo_ref[MIX_W:MIX_W + MEM_W, :], _TN,
                             preferred_element_type=F32))
    y = _layer_norm(alpha * x_ref[...] + sub, ln_ref[0:1, :], ln_ref[1:2, :])
    xo_ref[...] = y
    logitsT = lax.dot_general(wrT_ref[...], y.astype(BF16), _NT, preferred_element_type=F32)
    e = jnp.exp(logitsT - jnp.max(logitsT, axis=0, keepdims=True))
    affT_ref[...] = e / jnp.sum(e, axis=0, keepdims=True)


def _oproj(mixT, memT, x2d, wo, ln, wrT, alpha, batch, seq):
    n, d = x2d.shape
    tm = min(512, seq)
    nb = seq // tm
    kern = functools.partial(_oproj_kernel, alpha=alpha)
    const = lambda b, i: (0, 0)
    row = lambda b, i: (b * nb + i, 0)
    return pl.pallas_call(
        kern,
        grid=(batch, nb),
        in_specs=[
            pl.BlockSpec((1, MIX_W, tm), lambda b, i: (b, 0, i)),
            pl.BlockSpec((1, MEM_W, tm), lambda b, i: (b, 0, i)),
            pl.BlockSpec((tm, d), row),
            pl.BlockSpec(wo.shape, const),
            pl.BlockSpec((2, d), const),
            pl.BlockSpec(wrT.shape, const),
        ],
        out_specs=[
            pl.BlockSpec((tm, d), row),
            pl.BlockSpec((N_EXPERTS, tm), lambda b, i: (0, b * nb + i)),
        ],
        out_shape=[
            jax.ShapeDtypeStruct((n, d), F32),
            jax.ShapeDtypeStruct((N_EXPERTS, n), F32),
        ],
        compiler_params=_cparams(2),
        name="oproj_ln_router",
    )(mixT, memT, x2d, wo, ln, wrT)


def _route_select_kernel(affT_ref, tri_ref, ones_ref, low_ref, posm_ref, off_ref, *, cap):
    a = affT_ref[0]
    bits = lax.bitcast_convert_type(a, jnp.int32)

    def count(mask):
        return jnp.sum(jnp.sum(mask.astype(F32), axis=1, keepdims=True), axis=0, keepdims=True)

    def bisect(i, t):
        cand = t | jnp.left_shift(jnp.int32(1), 30 - i)
        return jnp.where(count(bits >= cand) >= cap, cand, t)

    t = lax.fori_loop(0, 31, bisect, jnp.zeros((1, 1), jnp.int32))

    def excl_cumsum(mask):
        mb = mask.astype(BF16)
        incl = jnp.dot(mb, tri_ref[...], preferred_element_type=F32)
        tot = jnp.dot(mb, ones_ref[...], preferred_element_type=F32)
        before = jnp.dot(low_ref[...], tot.astype(BF16), preferred_element_type=F32)
        return before + incl - mask.astype(F32), before

    gt = bits > t
    eq = bits == t
    need = cap - count(gt)
    rank, _ = excl_cumsum(eq)
    sel = gt | (eq & (rank < need))
    pos, before = excl_cumsum(sel)
    posm_ref[0] = jnp.where(sel, pos, -1.0).astype(jnp.int32)
    off_ref[0] = before[:, 0:1].astype(jnp.int32)


def _route_select(affT3, cap):
    e, r, _ = affT3.shape
    tri = (jnp.arange(LANES)[:, None] <= jnp.arange(LANES)[None, :]).astype(BF16)
    ones = jnp.ones((LANES, LANES), BF16)
    low = (jnp.arange(r)[None, :] < jnp.arange(r)[:, None]).astype(BF16)
    posm, off = pl.pallas_call(
        functools.partial(_route_select_kernel, cap=cap),
        grid=(e,),
        in_specs=[pl.BlockSpec((1, r, LANES), lambda ei: (ei, 0, 0)),
                  pl.BlockSpec((LANES, LANES), lambda ei: (0, 0)),
                  pl.BlockSpec((LANES, LANES), lambda ei: (0, 0)),
                  pl.BlockSpec((r, r), lambda ei: (0, 0))],
        out_specs=[pl.BlockSpec((1, r, LANES), lambda ei: (ei, 0, 0)),
                   pl.BlockSpec((1, r, 1), lambda ei: (ei, 0, 0))],
        out_shape=[jax.ShapeDtypeStruct((e, r, LANES), jnp.int32),
                   jax.ShapeDtypeStruct((e, r, 1), jnp.int32)],
        compiler_params=_cparams(1),
        name="route_select",
    )(affT3, tri, ones, low)
    return posm, off.reshape(e, r)


COMPACT_ROWS = 16


def _route_compact_kernel(off_ref, posm_ref, affT_ref, idx_ref, gate_ref, acc_ref, *, nrows, ncb):
    e = pl.program_id(0)
    acc_ref[...] = jnp.zeros_like(acc_ref)
    slot = lax.broadcasted_iota(jnp.int32, (LANES, LANES), 0)
    sub = lax.broadcasted_iota(jnp.int32, (COMPACT_ROWS, LANES), 0)
    lane = lax.broadcasted_iota(jnp.int32, (COMPACT_ROWS, LANES), 1).astype(F32)

    def body(r, carry):
        cb = lax.shift_right_logical(off_ref[e, r], 7)
        rel = posm_ref[0, pl.ds(r, 1), :] - cb * LANES
        a = affT_ref[0, pl.ds(r, 1), :]
        hi = a.astype(BF16).astype(F32)
        mid = (a - hi).astype(BF16).astype(F32)
        lo = a - hi - mid
        vals = jnp.where(sub == 0, lane, jnp.where(sub == 1, jnp.asarray(r, F32), jnp.where(
            sub == 2, hi, jnp.where(sub == 3, mid, jnp.where(sub == 4, lo, 0.0))))).astype(BF16)
        for part in range(2):
            hit = ((rel - part * LANES) == slot).astype(BF16)
            acc_ref[cb + part] += lax.dot_general(vals, hit, _NT, preferred_element_type=F32)
        return carry

    lax.fori_loop(0, nrows, body, 0, unroll=8)
    acc = acc_ref[0:ncb]
    idx_ref[0] = (acc[:, 1:2, :] * LANES + acc[:, 0:1, :]).astype(jnp.int32)
    gate_ref[0] = acc[:, 2:3, :] + acc[:, 3:4, :] + acc[:, 4:5, :]


def _route_compact(off, posm, affT3, cap):
    e, r, _ = posm.shape
    ncb = cap // LANES
    idx, gate = pl.pallas_call(
        functools.partial(_route_compact_kernel, nrows=r, ncb=ncb),
        grid=(e,),
        in_specs=[pl.BlockSpec(memory_space=pltpu.SMEM),
                  pl.BlockSpec((1, r, LANES), lambda ei: (ei, 0, 0)),
                  pl.BlockSpec((1, r, LANES), lambda ei: (ei, 0, 0))],
        out_specs=[pl.BlockSpec((1, ncb, 1, LANES), lambda ei: (ei, 0, 0, 0)),
                   pl.BlockSpec((1, ncb, 1, LANES), lambda ei: (ei, 0, 0, 0))],
        out_shape=[jax.ShapeDtypeStruct((e, ncb, 1, LANES), jnp.int32),
                   jax.ShapeDtypeStruct((e, ncb, 1, LANES), F32)],
        scratch_shapes=[pltpu.VMEM((ncb + 2, COMPACT_ROWS, LANES), F32)],
        compiler_params=_cparams(1),
        name="route_compact",
    )(off, posm, affT3)
    return idx.reshape(e, cap), gate.reshape(e, cap, 1)


FFN_ROWS = 512


def _ffn_kernel(idx_ref, nxt_ref, x_hbm, wg_ref, wu_ref, wd_ref, gate_ref, y_ref,
                xf_ref, xb_ref, acc_ref, sem, *, nf, nblk):
    f = pl.program_id(2)
    blk = pl.program_id(0) * pl.num_programs(1) + pl.program_id(1)
    slot = blk % 2
    tc = xb_ref.shape[0]
    tcp = xf_ref.shape[1]
    per_step = tcp // nf
    chunk = min(FFN_ROWS, tc)

    def row_copy(ids, i, s):
        return pltpu.make_async_copy(x_hbm.at[pl.ds(ids[0, 0, i], 1)], xf_ref.at[s, pl.ds(i, 1)], sem.at[s])

    def all_rows(ids, s, fn):
        def one(i, c):
            fn(row_copy(ids, i, s))
            return c
        lax.fori_loop(0, tcp, one, 0, unroll=8)

    @pl.when(f == 0)
    def _():
        @pl.when(blk == 0)
        def _():
            all_rows(idx_ref, slot, lambda c: c.start())

        all_rows(idx_ref, slot, lambda c: c.wait())
        for c in range(tc // chunk):
            rows = slice(c * chunk, (c + 1) * chunk)
            xb_ref[rows, :] = xf_ref[slot, rows, :].astype(BF16)
        acc_ref[...] = jnp.zeros_like(acc_ref)

    wg = wg_ref[0, 0].astype(BF16)
    wu = wu_ref[0, 0].astype(BF16)
    wd = wd_ref[0, 0].astype(BF16)
    for c in range(tc // chunk):
        rows = slice(c * chunk, (c + 1) * chunk)
        x = xb_ref[rows, :]
        g = jnp.dot(x, wg, preferred_element_type=F32)
        u = jnp.dot(x, wu, preferred_element_type=F32)
        h = (g / (1.0 + jnp.exp(-g))) * u
        acc_ref[rows, :] += jnp.dot(h.astype(BF16), wd, preferred_element_type=F32)

    for j in range(per_step):
        row_copy(nxt_ref, f * per_step + j, 1 - slot).start()

    @pl.when(f == nf - 1)
    def _():
        y_ref[0] = (acc_ref[...] * gate_ref[0]).astype(y_ref.dtype)

        @pl.when(blk == nblk - 1)
        def _():
            all_rows(nxt_ref, 1 - slot, lambda c: c.wait())


def _expert_ffn(idx, x2d, w_gu, w_dn, layer, gates):
    e, c = idx.shape
    d = x2d.shape[1]
    fdim = w_dn.shape[2]
    tc = min(2048, c)
    ncb = c // tc
    nblk = e * ncb
    tf = 256
    nf = fdim // tf
    assert tc % min(FFN_ROWS, tc) == 0
    per_step = -(-tc // nf)
    per_step += -per_step % 8
    tcp = per_step * nf
    kern = functools.partial(_ffn_kernel, nf=nf, nblk=nblk)
    idx3 = jnp.pad(idx.reshape(nblk, 1, tc), ((0, 0), (0, 0), (0, tcp - tc)))
    return pl.pallas_call(
        kern,
        grid=(e, ncb, nf),
        in_specs=[
            pl.BlockSpec((1, 1, tcp), lambda ei, ci, fi: (ei * ncb + ci, 0, 0), memory_space=pltpu.SMEM),
            pl.BlockSpec((1, 1, tcp), lambda ei, ci, fi: (jnp.minimum(ei * ncb + ci + 1, nblk - 1), 0, 0),
                         memory_space=pltpu.SMEM),
            pl.BlockSpec(memory_space=pl.ANY),
            pl.BlockSpec((1, 1, d, tf), lambda ei, ci, fi: (layer, ei, 0, fi)),
            pl.BlockSpec((1, 1, d, tf), lambda ei, ci, fi: (layer, ei, 0, fi + nf)),
            pl.BlockSpec((1, 1, tf, d), lambda ei, ci, fi: (layer, ei, fi, 0)),
            pl.BlockSpec((1, tc, 1), lambda ei, ci, fi: (ei, ci, 0)),
        ],
        out_specs=pl.BlockSpec((1, tc, d), lambda ei, ci, fi: (ei, ci, 0)),
        out_shape=jax.ShapeDtypeStruct((e, c, d), BF16),
        scratch_shapes=[pltpu.VMEM((2, tcp, d), F32), pltpu.VMEM((tc, d), BF16), pltpu.VMEM((tc, d), F32),
                        pltpu.SemaphoreType.DMA((2,))],
        compiler_params=_cparams(3),
        name="expert_ffn",
    )(idx3, idx3, x2d, w_gu, w_gu, w_dn, gates)


COMBINE_ROWS = 8
BF16_ROWS = 16
WIN = LANES + BF16_ROWS
WIN_SHORT = 3 * BF16_ROWS


def _combine_kernel(off_ref, x_ref, posm_ref, y_hbm, ln_ref, o_ref, ybuf, sem, *, alpha, cap):
    t = pl.program_id(0)
    n_exp = posm_ref.shape[0]
    d = x_ref.shape[1]
    slot = lax.broadcasted_iota(jnp.int32, (WIN, LANES), 0)

    @pl.when(t == 0)
    def _():
        ybuf[...] = jnp.zeros_like(ybuf)

    def win_start(rr, e):
        start = lax.shift_right_logical(off_ref[e, t * COMBINE_ROWS + rr], 4) * BF16_ROWS
        return pl.multiple_of(jnp.minimum(start, cap - WIN), BF16_ROWS)

    def for_each_window(rr, par, fn):
        starts = [win_start(rr, e) for e in range(n_exp)]
        short = starts[0] >= 0
        for e in range(n_exp):
            short = short & (off_ref[e, t * COMBINE_ROWS + rr + 1] - starts[e] <= WIN_SHORT)
        for cond, rows in ((short, WIN_SHORT), (jnp.logical_not(short), WIN)):
            @pl.when(cond)
            def _():
                for e in range(n_exp):
                    fn(pltpu.make_async_copy(y_hbm.at[e, pl.ds(starts[e], rows)],
                                             ybuf.at[par, e, pl.ds(0, rows)], sem.at[par]))

    def row(rr, par):
        @pl.when(rr + 1 < COMBINE_ROWS)
        def _():
            for_each_window(rr + 1, 1 - par, lambda c: c.start())
        for_each_window(rr, par, lambda c: c.wait())
        hit = jnp.concatenate(
            [((posm_ref[e, pl.ds(rr, 1), :] - win_start(rr, e)) == slot).astype(BF16) for e in range(n_exp)],
            axis=0)
        tot = lax.dot_general(hit, ybuf[par].reshape(n_exp * WIN, d), _TN, preferred_element_type=F32)
        rows = pl.ds(pl.multiple_of(rr * LANES, LANES), LANES)
        o_ref[rows, :] = _layer_norm(alpha * x_ref[rows, :] + tot, ln_ref[0:1, :], ln_ref[1:2, :])

    for_each_window(0, 0, lambda c: c.start())

    def body(k, carry):
        row(2 * k, 0)
        row(2 * k + 1, 1)
        return carry

    lax.fori_loop(0, COMBINE_ROWS // 2, body, 0)


def _combine_ln(off, x2d, posm, y, ln, alpha, cap):
    n, d = x2d.shape
    e, r, _ = posm.shape
    rows = COMBINE_ROWS
    assert r % rows == 0 and rows % 2 == 0 and cap >= WIN
    tt = rows * LANES
    off = jnp.concatenate([off, jnp.full((e, 1), cap, jnp.int32)], axis=1)
    return pl.pallas_call(
        functools.partial(_combine_kernel, alpha=alpha, cap=cap),
        grid=(n // tt,),
        in_specs=[pl.BlockSpec(memory_space=pltpu.SMEM),
                  pl.BlockSpec((tt, d), lambda ti: (ti, 0)),
                  pl.BlockSpec((e, rows, LANES), lambda ti: (0, ti, 0)),
                  pl.BlockSpec(memory_space=pl.ANY),
                  pl.BlockSpec((2, d), lambda ti: (0, 0))],
        out_specs=pl.BlockSpec((tt, d), lambda ti: (ti, 0)),
        out_shape=jax.ShapeDtypeStruct((n, d), F32),
        scratch_shapes=[pltpu.VMEM((2, e, WIN, d), BF16), pltpu.SemaphoreType.DMA((2,))],
        compiler_params=_cparams(1),
        name="combine_ln",
    )(off, x2d, posm, y, ln)


def _rope_tables(seq):
    def tab(pos, dim):
        inv = ROPE_THETA ** (-jnp.arange(0, dim, 2, dtype=F32) / dim)
        ang = pos.astype(F32)[:, None] * inv[None, :]
        return jnp.cos(ang), jnp.sin(ang)

    c1, s1 = tab(jnp.arange(seq), HEAD_DIM)
    cos1 = jnp.concatenate([c1, c1], axis=-1)
    sin1 = jnp.concatenate([-s1, s1], axis=-1)
    cr, sr = tab(jnp.arange(seq) // GRID_W, HEAD_DIM // 2)
    cc, sc = tab(jnp.arange(seq) % GRID_W, HEAD_DIM // 2)
    cos2 = jnp.concatenate([cr, cr, cc, cc], axis=-1)
    sin2 = jnp.concatenate([-sr, sr, -sc, sc], axis=-1)
    rep = LANES // HEAD_DIM
    nat = lambda c, s: (jnp.tile(c, (1, rep)), jnp.tile(s, (1, rep)))
    return ((cos1.T, sin1.T), nat(cos1, sin1)), ((cos2.T, sin2.T), nat(cos2, sin2))


def _prep_params(p):
    bf = lambda w: w.astype(BF16)
    tr = lambda w: jnp.swapaxes(w, -1, -2).astype(BF16)
    a, bw, cw = p["a_w_in"], p["b_w_in"], p["c_w_in"]
    out = dict(p)
    out.update(
        a_wqT=tr(a[:, :, 0:MIX_W]), a_wk=bf(a[:, :, MIX_W:2 * MIX_W]),
        a_wvT=tr(a[:, :, 2 * MIX_W:3 * MIX_W]), a_wmT=tr(a[:, :, 3 * MIX_W:]),
        w_memk=bf(p["w_mem_kv"][:, :, 0:MEM_W]), w_memvT=tr(p["w_mem_kv"][:, :, MEM_W:]),
        w_o=bf(p["w_o"]), w_routerT=tr(p["w_router"]),
    )
    for name, w in (("b", bw), ("c", cw)):
        out[name + "_wqT"] = tr(w[:, :, 0:MIX_W])
        out[name + "_wk"] = bf(w[:, :, MIX_W:MIX_W + KV_W])
        out[name + "_wvT"] = tr(w[:, :, MIX_W + KV_W:MIX_W + 2 * KV_W])
        out[name + "_wmT"] = tr(w[:, :, MIX_W + 2 * KV_W:])
    return out


def _encode(x, mem, p):
    b, s, d = x.shape
    n = b * s
    m_len = mem.shape[1]
    depth = p["w_o"].shape[0]
    alpha = (2 * depth) ** 0.25
    (tabs1T, tabs1), (tabs2T, tabs2) = _rope_tables(s)
    bd = (jnp.arange(LANES)[:, None] // HEAD_DIM == jnp.arange(LANES)[None, :] // HEAD_DIM).astype(BF16)
    ones_g = jnp.ones((1, LANES), F32)
    ones_gT = jnp.ones((HEAD_DIM, LANES), F32)
    cap = EC_FACTOR * n // N_EXPERTS
    x2d = x.reshape(n, d)
    mem2d = mem.reshape(b * m_len, d)
    tq_full, tk_full = 512, 512

    for i in range(depth):
        kind, j = i % N_MIXERS, i // N_MIXERS
        common = dict(batch=b, seq=s)
        if kind == 0:
            qT, k4, vT5, qmT = _inproj(x2d, p["a_wqT"][j], p["a_wvT"][j], p["a_wmT"][j], p["a_wk"][j],
                                       tabs1T, tabs1, ones_gT, ones_g, bd, half=HEAD_DIM // 2,
                                       qk_norm=False, vchunk=tk_full, **common)
            lam_init = 0.8 - 0.6 * math.exp(-0.3 * i)
            tk = min(tk_full, s)
            k5 = k4.reshape(b, 2 * A_HEADS, s // tk, tk, HEAD_DIM)
            g_col = jnp.broadcast_to(p["a_subln"][j][:, None], (A_V_DIM, LANES))
            mixT = _diff_attention(qT, k5, vT5, p["a_lambda"][j], g_col, lam_init, 2 * tq_full)
        elif kind == 1:
            qT, k4, vT5, qmT = _inproj(x2d, p["b_wqT"][j], p["b_wvT"][j], p["b_wmT"][j], p["b_wk"][j],
                                       tabs1T, tabs1, ones_gT, ones_g, bd, half=HEAD_DIM // 2,
                                       qk_norm=False, vchunk=LANES, **common)
            k5 = k4.reshape(b, GQA_KV_HEADS, s // LANES, LANES, HEAD_DIM)
            mixT = _window_attention(qT, k5, vT5, p["b_sink"][j], GQA_GROUP, tq_full)
        else:
            gq = jnp.broadcast_to(p["c_qk_norm"][j][0][:, None], (HEAD_DIM, LANES))
            gk = jnp.tile(p["c_qk_norm"][j][1], LANES // HEAD_DIM)[None, :]
            qT, k4, vT5, qmT = _inproj(x2d, p["c_wqT"][j], p["c_wvT"][j], p["c_wmT"][j], p["c_wk"][j],
                                       tabs2T, tabs2, gq, gk, bd, half=HEAD_DIM // 4,
                                       qk_norm=True, vchunk=tk_full // 2, **common)
            tk = min(tk_full // 2, s)
            k5 = k4.reshape(b, GQA_KV_HEADS, s // tk, tk, HEAD_DIM)
            mixT = _gqa_attention(qT, k5, vT5, GQA_GROUP, 2 * tq_full)

        km, vmT = _memproj(mem2d, p["w_memk"][i], p["w_memvT"][i], b, m_len)
        memT = _gqa_attention(qmT, km.reshape(b, MEM_HEADS, 1, m_len, HEAD_DIM), vmT, 1, 2 * tq_full)

        x2d, affT = _oproj(mixT, memT, x2d, p["w_o"][i], p["ln_mix"][i], p["w_routerT"][i], alpha, b, s)

        affT3 = affT.reshape(N_EXPERTS, n // LANES, LANES)
        posm, off = _route_select(affT3, cap)
        idx, gates = _route_compact(off, posm, affT3, cap)
        y = _expert_ffn(idx, x2d, p["w_gate_up"], p["w_down"], i, gates)
        x2d = _combine_ln(off, x2d, posm, y, p["ln_ffn"][i], alpha, cap)
    return x2d.reshape(b, s, d)


def kernel(x_prompt, x_sample, mem_prompt, mem_sample, a_w_in, a_lambda, a_subln, b_w_in, b_sink, c_w_in, c_qk_norm, w_mem_kv, w_o, ln_mix, w_router, w_gate_up, w_down, ln_ffn):
    p = _prep_params(dict(
        a_w_in=a_w_in, a_lambda=a_lambda, a_subln=a_subln, b_w_in=b_w_in, b_sink=b_sink,
        c_w_in=c_w_in, c_qk_norm=c_qk_norm, w_mem_kv=w_mem_kv, w_o=w_o, ln_mix=ln_mix,
        w_router=w_router, w_gate_up=w_gate_up, w_down=w_down, ln_ffn=ln_ffn))
    return (_encode(x_prompt, mem_prompt, p), _encode(x_sample, mem_sample, p))
```

```python
import functools
import math

import jax
import jax.numpy as jnp
from jax import lax
from jax.experimental import pallas as pl
from jax.experimental.pallas import tpu as pltpu

F32 = jnp.float32
BF16 = jnp.bfloat16

HEAD_DIM = 64
GQA_KV_HEADS = 4
GQA_GROUP = 3
A_HEADS = 6
A_V_DIM = 2 * HEAD_DIM
MEM_HEADS = 4
MIX_W = 768
MEM_W = MEM_HEADS * HEAD_DIM
KV_W = GQA_KV_HEADS * HEAD_DIM
N_MIXERS = 3
WINDOW = 128
GRID_W = 64
ROPE_THETA = 10000.0
N_EXPERTS = 16
EC_FACTOR = 2
LN_EPS = 1e-5
RMS_EPS = 1e-6
LOG2E = 1.4426950408889634
QK_SCALE = HEAD_DIM ** -0.5 * LOG2E
LANES = 128
ONES_ROWS = 16
ATTN_UNROLL = 4
VMEM_LIMIT = 52 * 1024 * 1024

_NT = (((1,), (1,)), ((), ()))
_TN = (((0,), (0,)), ((), ()))


def _cparams(grid_rank):
    return pltpu.CompilerParams(dimension_semantics=("arbitrary",) * grid_rank,
                                vmem_limit_bytes=VMEM_LIMIT)


def _layer_norm(h, g, b):
    mu = jnp.mean(h, axis=-1, keepdims=True)
    d = h - mu
    var = jnp.mean(d * d, axis=-1, keepdims=True)
    return d * lax.rsqrt(var + LN_EPS) * g + b


def _lane_tile(a, width):
    return jnp.concatenate([a] * (width // a.shape[1]), axis=1)


def _inproj_kernel(x_ref, wqT_ref, wvT_ref, wmT_ref, wk_ref, cosT_ref, sinT_ref, cos_ref, sin_ref,
                   gT_ref, g_ref, bd_ref, qT_ref, k_ref, vT_ref, qmT_ref, *, half, qk_norm, vchunk):
    xb = x_ref[...].astype(BF16)
    tm = xb.shape[0]
    cosT = cosT_ref[...]
    sinT = sinT_ref[...]
    nparts = HEAD_DIM // half

    yqT = lax.dot_general(wqT_ref[...], xb, _NT, preferred_element_type=F32)
    for h in range(yqT.shape[0] // HEAD_DIM):
        xh = yqT[h * HEAD_DIM:(h + 1) * HEAD_DIM]
        if qk_norm:
            ms = jnp.mean(xh * xh, axis=0, keepdims=True)
            xh = xh * lax.rsqrt(ms + RMS_EPS) * _lane_tile(gT_ref[...], tm)
        rot = jnp.concatenate([xh[(p ^ 1) * half:((p ^ 1) + 1) * half] for p in range(nparts)], axis=0)
        qT_ref[0, h * HEAD_DIM:(h + 1) * HEAD_DIM, :] = ((xh * cosT + rot * sinT) * QK_SCALE).astype(BF16)

    yvT = lax.dot_general(wvT_ref[...], xb, _NT, preferred_element_type=F32).astype(BF16)
    for c in range(tm // vchunk):
        vT_ref[0, c] = yvT[:, c * vchunk:(c + 1) * vchunk]
    ymT = lax.dot_general(wmT_ref[...], xb, _NT, preferred_element_type=F32)
    qmT_ref[0] = (ymT * QK_SCALE).astype(BF16)

    lane = lax.broadcasted_iota(jnp.int32, (tm, LANES), 1)
    first = (lane % (2 * half)) < half
    cos = cos_ref[...]
    sin = sin_ref[...]
    yk = jnp.dot(xb, wk_ref[...], preferred_element_type=F32)
    for c in range(yk.shape[1] // LANES):
        ch = yk[:, c * LANES:(c + 1) * LANES]
        if qk_norm:
            sq = ch * ch
            hi = sq.astype(BF16)
            lo = (sq - hi.astype(F32)).astype(BF16)
            ss = (jnp.dot(hi, bd_ref[...], preferred_element_type=F32)
                  + jnp.dot(lo, bd_ref[...], preferred_element_type=F32))
            ch = ch * lax.rsqrt(ss * (1.0 / HEAD_DIM) + RMS_EPS) * g_ref[...]
        rot = jnp.where(first, pltpu.roll(ch, LANES - half, 1), pltpu.roll(ch, half, 1))
        ch = (ch * cos + rot * sin).astype(BF16)
        k_ref[0, 2 * c] = ch[:, 0:HEAD_DIM]
        k_ref[0, 2 * c + 1] = ch[:, HEAD_DIM:LANES]


def _inproj(x2d, wqT, wvT, wmT, wk, tabsT, tabs, gT, g, bd, *, batch, seq, half, qk_norm, vchunk):
    n, d = x2d.shape
    tm = min(512, seq)
    vchunk = min(vchunk, tm)
    nb = seq // tm
    qw, vw, kw = wqT.shape[0], wvT.shape[0], wk.shape[1]
    kern = functools.partial(_inproj_kernel, half=half, qk_norm=qk_norm, vchunk=vchunk)
    const = lambda b, i: (0, 0)
    return pl.pallas_call(
        kern,
        grid=(batch, nb),
        in_specs=[
            pl.BlockSpec((tm, d), lambda b, i: (b * nb + i, 0)),
            pl.BlockSpec(wqT.shape, const),
            pl.BlockSpec(wvT.shape, const),
            pl.BlockSpec(wmT.shape, const),
            pl.BlockSpec(wk.shape, const),
            pl.BlockSpec((HEAD_DIM, tm), lambda b, i: (0, i)),
            pl.BlockSpec((HEAD_DIM, tm), lambda b, i: (0, i)),
            pl.BlockSpec((tm, LANES), lambda b, i: (i, 0)),
            pl.BlockSpec((tm, LANES), lambda b, i: (i, 0)),
            pl.BlockSpec((HEAD_DIM, LANES), const),
            pl.BlockSpec((1, LANES), const),
            pl.BlockSpec((LANES, LANES), const),
        ],
        out_specs=[
            pl.BlockSpec((1, qw, tm), lambda b, i: (b, 0, i)),
            pl.BlockSpec((1, kw // HEAD_DIM, tm, HEAD_DIM), lambda b, i: (b, 0, i, 0)),
            pl.BlockSpec((1, tm // vchunk, vw, vchunk), lambda b, i: (b, i, 0, 0)),
            pl.BlockSpec((1, MEM_W, tm), lambda b, i: (b, 0, i)),
        ],
        out_shape=[
            jax.ShapeDtypeStruct((batch, qw, seq), BF16),
            jax.ShapeDtypeStruct((batch, kw // HEAD_DIM, seq, HEAD_DIM), BF16),
            jax.ShapeDtypeStruct((batch, seq // vchunk, vw, vchunk), BF16),
            jax.ShapeDtypeStruct((batch, MEM_W, seq), BF16),
        ],
        compiler_params=_cparams(2),
        name="inproj",
    )(x2d, wqT, wvT, wmT, wk, *tabsT, *tabs, gT, g, bd)


def _memproj_kernel(x_ref, wk_ref, wvT_ref, k_ref, vT_ref):
    xb = x_ref[...].astype(BF16)
    yk = jnp.dot(xb, wk_ref[...], preferred_element_type=F32).astype(BF16)
    for h in range(MEM_HEADS):
        k_ref[0, h] = yk[:, h * HEAD_DIM:(h + 1) * HEAD_DIM]
    vT_ref[0, 0] = lax.dot_general(wvT_ref[...], xb, _NT, preferred_element_type=F32).astype(BF16)


def _memproj(mem2d, wk, wvT, batch, m_len):
    d = mem2d.shape[1]
    return pl.pallas_call(
        _memproj_kernel,
        grid=(batch,),
        in_specs=[pl.BlockSpec((m_len, d), lambda b: (b, 0)),
                  pl.BlockSpec(wk.shape, lambda b: (0, 0)),
                  pl.BlockSpec(wvT.shape, lambda b: (0, 0))],
        out_specs=[pl.BlockSpec((1, MEM_HEADS, m_len, HEAD_DIM), lambda b: (b, 0, 0, 0)),
                   pl.BlockSpec((1, 1, MEM_W, m_len), lambda b: (b, 0, 0, 0))],
        out_shape=[jax.ShapeDtypeStruct((batch, MEM_HEADS, m_len, HEAD_DIM), BF16),
                   jax.ShapeDtypeStruct((batch, 1, MEM_W, m_len), BF16)],
        compiler_params=_cparams(1),
        name="memproj",
    )(mem2d, wk, wvT)


def _ones_rows(tk):
    return (lax.broadcasted_iota(jnp.int32, (ONES_ROWS, tk), 0) == 0).astype(BF16)


def _flash_step(sT, va, m, acc):
    m_new = jnp.maximum(m, jnp.max(sT, axis=0, keepdims=True))
    a = jnp.exp2(m - m_new)
    pT = jnp.exp2(sT - m_new).astype(BF16)
    return m_new, a * acc + jnp.dot(va, pT, preferred_element_type=F32)


def _gqa_kernel(qT_ref, k_ref, vT_ref, o_ref, *, group, nk, unroll):
    tk = k_ref.shape[3]
    qT = jnp.concatenate([qT_ref[0, g * HEAD_DIM:(g + 1) * HEAD_DIM, :] for g in range(group)], axis=1)
    nq = qT.shape[1]
    tq = nq // group
    ones = _ones_rows(tk)

    def body(j, carry):
        sT = jnp.dot(k_ref[0, 0, j], qT, preferred_element_type=F32)
        va = jnp.concatenate([vT_ref[0, j], ones], axis=0)
        return _flash_step(sT, va, *carry)

    init = (jnp.full((1, nq), -jnp.inf, F32), jnp.zeros((HEAD_DIM + ONES_ROWS, nq), F32))
    _, acc = lax.fori_loop(0, nk, body, init, unroll=unroll)
    o = acc[0:HEAD_DIM] / acc[HEAD_DIM:HEAD_DIM + 1]
    for g in range(group):
        o_ref[0, g * HEAD_DIM:(g + 1) * HEAD_DIM, :] = o[:, g * tq:(g + 1) * tq].astype(o_ref.dtype)


def _gqa_attention(qT, k5, vT5, group, tq):
    b, qrows, s = qT.shape
    _, kvh, nk, tk, _ = k5.shape
    tq = min(tq, s)
    kern = functools.partial(_gqa_kernel, group=group, nk=nk, unroll=math.gcd(nk, 2 * ATTN_UNROLL))
    return pl.pallas_call(
        kern,
        grid=(b, kvh, s // tq),
        in_specs=[
            pl.BlockSpec((1, group * HEAD_DIM, tq), lambda bi, h, i: (bi, h, i)),
            pl.BlockSpec((1, 1, nk, tk, HEAD_DIM), lambda bi, h, i: (bi, h, 0, 0, 0)),
            pl.BlockSpec((1, nk, HEAD_DIM, tk), lambda bi, h, i: (bi, 0, h, 0)),
        ],
        out_specs=pl.BlockSpec((1, group * HEAD_DIM, tq), lambda bi, h, i: (bi, h, i)),
        out_shape=jax.ShapeDtypeStruct(qT.shape, BF16),
        compiler_params=_cparams(3),
        name="gqa_attention",
    )(qT, k5, vT5)


def _diff_kernel(lam_ref, g_ref, qT_ref, k_ref, vT_ref, o_ref, *, nk, unroll, lam_init):
    tk = k_ref.shape[3]
    tq = qT_ref.shape[2]
    lp = lam_ref[...]
    lam = (jnp.exp(jnp.sum(lp[0:1] * lp[1:2], axis=-1, keepdims=True))
           - jnp.exp(jnp.sum(lp[2:3] * lp[3:4], axis=-1, keepdims=True)) + lam_init)
    q0T = qT_ref[0, 0:HEAD_DIM, :]
    q1T = qT_ref[0, HEAD_DIM:2 * HEAD_DIM, :]
    ones = _ones_rows(tk)

    def body(j, carry):
        sT = jnp.concatenate([jnp.dot(k_ref[0, 0, j], q0T, preferred_element_type=F32),
                              jnp.dot(k_ref[0, 1, j], q1T, preferred_element_type=F32)], axis=1)
        va = jnp.concatenate([vT_ref[0, j], ones], axis=0)
        return _flash_step(sT, va, *carry)

    init = (jnp.full((1, 2 * tq), -jnp.inf, F32), jnp.zeros((A_V_DIM + ONES_ROWS, 2 * tq), F32))
    _, acc = lax.fori_loop(0, nk, body, init, unroll=unroll)
    o = acc[0:A_V_DIM] / acc[A_V_DIM:A_V_DIM + 1]
    o = o[:, 0:tq] - lam * o[:, tq:2 * tq]
    ms = jnp.mean(o * o, axis=0, keepdims=True)
    o = o * lax.rsqrt(ms + RMS_EPS) * _lane_tile(g_ref[...], tq) * (1.0 - lam_init)
    o_ref[0] = o.astype(o_ref.dtype)


def _diff_attention(qT, k5, vT5, lam_p, g_col, lam_init, tq):
    b, qrows, s = qT.shape
    _, _, nk, tk, _ = k5.shape
    tq = min(tq, s)
    kern = functools.partial(_diff_kernel, nk=nk, unroll=math.gcd(nk, ATTN_UNROLL), lam_init=lam_init)
    return pl.pallas_call(
        kern,
        grid=(b, A_HEADS, s // tq),
        in_specs=[
            pl.BlockSpec((4, HEAD_DIM), lambda bi, h, i: (0, 0)),
            pl.BlockSpec((A_V_DIM, LANES), lambda bi, h, i: (0, 0)),
            pl.BlockSpec((1, 2 * HEAD_DIM, tq), lambda bi, h, i: (bi, h, i)),
            pl.BlockSpec((1, 2, nk, tk, HEAD_DIM), lambda bi, h, i: (bi, h, 0, 0, 0)),
            pl.BlockSpec((1, nk, A_V_DIM, tk), lambda bi, h, i: (bi, 0, h, 0)),
        ],
        out_specs=pl.BlockSpec((1, A_V_DIM, tq), lambda bi, h, i: (bi, h, i)),
        out_shape=jax.ShapeDtypeStruct(qT.shape, BF16),
        compiler_params=_cparams(3),
        name="diff_attention",
    )(lam_p, g_col, qT, k5, vT5)


def _window_kernel(sink_ref, qT_ref, k_ref, vT_ref, o_ref, *, group, nwc, nchunks):
    h = pl.program_id(1)
    tq = qT_ref.shape[2]
    nq = group * tq
    q0 = pl.program_id(2) * tq
    c0 = jnp.clip(q0 // LANES - WINDOW // LANES, 0, nchunks - nwc)
    qT = jnp.concatenate([qT_ref[0, g * HEAD_DIM:(g + 1) * HEAD_DIM, :] for g in range(group)], axis=1)
    qpos = q0 + lax.broadcasted_iota(jnp.int32, (LANES, nq), 1) % tq
    krow = lax.broadcasted_iota(jnp.int32, (LANES, nq), 0)
    sink = jnp.concatenate(
        [jnp.full((1, tq), sink_ref[h * group + g] * LOG2E, F32) for g in range(group)], axis=1)
    ones = _ones_rows(LANES)
    scores = []
    m = sink
    for c in range(nwc):
        sT = jnp.dot(k_ref[0, 0, c0 + c], qT, preferred_element_type=F32)
        kpos = (c0 + c) * LANES + krow
        sT = jnp.where(jnp.abs(kpos - qpos) <= WINDOW, sT, -jnp.inf)
        scores.append(sT)
        m = jnp.maximum(m, jnp.max(sT, axis=0, keepdims=True))
    acc = jnp.zeros((HEAD_DIM + ONES_ROWS, nq), F32)
    for c in range(nwc):
        va = jnp.concatenate([vT_ref[0, c0 + c], ones], axis=0)
        acc = acc + jnp.dot(va, jnp.exp2(scores[c] - m).astype(BF16), preferred_element_type=F32)
    o = acc[0:HEAD_DIM] / (acc[HEAD_DIM:HEAD_DIM + 1] + jnp.exp2(sink - m))
    for g in range(group):
        o_ref[0, g * HEAD_DIM:(g + 1) * HEAD_DIM, :] = o[:, g * tq:(g + 1) * tq].astype(o_ref.dtype)


def _window_attention(qT, k5, vT5, sink, group, tq):
    b, qrows, s = qT.shape
    _, kvh, nchunks, tk, _ = k5.shape
    tq = min(tq, s)
    nwc = min(nchunks, tq // LANES + 2 * (WINDOW // LANES))
    kern = functools.partial(_window_kernel, group=group, nwc=nwc, nchunks=nchunks)
    return pl.pallas_call(
        kern,
        grid=(b, kvh, s // tq),
        in_specs=[
            pl.BlockSpec(memory_space=pltpu.SMEM),
            pl.BlockSpec((1, group * HEAD_DIM, tq), lambda bi, h, i: (bi, h, i)),
            pl.BlockSpec((1, 1, nchunks, tk, HEAD_DIM), lambda bi, h, i: (bi, h, 0, 0, 0)),
            pl.BlockSpec((1, nchunks, HEAD_DIM, tk), lambda bi, h, i: (bi, 0, h, 0)),
        ],
        out_specs=pl.BlockSpec((1, group * HEAD_DIM, tq), lambda bi, h, i: (bi, h, i)),
        out_shape=jax.ShapeDtypeStruct(qT.shape, BF16),
        compiler_params=_cparams(3),
        name="window_attention",
    )(sink, qT, k5, vT5)


def _oproj_kernel(mixT_ref, memT_ref, x_ref, wo_ref, ln_ref, wrT_ref, xo_ref, affT_ref, *, alpha):
    sub = (lax.dot_general(mixT_ref[0], wo_ref[0:MIX_W, :], _TN, preferred_element_type=F32)
           + lax.dot_general(memT_ref[0], wo_ref[MIX_W:MIX_W + MEM_W, :], _TN,
                             preferred_element_type=F32))
    y = _layer_norm(alpha * x_ref[...] + sub, ln_ref[0:1, :], ln_ref[1:2, :])
    xo_ref[...] = y
    logitsT = lax.dot_general(wrT_ref[...], y.astype(BF16), _NT, preferred_element_type=F32)
    e = jnp.exp(logitsT - jnp.max(logitsT, axis=0, keepdims=True))
    affT_ref[...] = e / jnp.sum(e, axis=0, keepdims=True)


def _oproj(mixT, memT, x2d, wo, ln, wrT, alpha, batch, seq):
    n, d = x2d.shape
    tm = min(512, seq)
    nb = seq // tm
    kern = functools.partial(_oproj_kernel, alpha=alpha)
    const = lambda b, i: (0, 0)
    row = lambda b, i: (b * nb + i, 0)
    return pl.pallas_call(
        kern,
        grid=(batch, nb),
        in_specs=[
            pl.BlockSpec((1, MIX_W, tm), lambda b, i: (b, 0, i)),
            pl.BlockSpec((1, MEM_W, tm), lambda b, i: (b, 0, i)),
            pl.BlockSpec((tm, d), row),
            pl.BlockSpec(wo.shape, const),
            pl.BlockSpec((2, d), const),
            pl.BlockSpec(wrT.shape, const),
        ],
        out_specs=[
            pl.BlockSpec((tm, d), row),
            pl.BlockSpec((N_EXPERTS, tm), lambda b, i: (0, b * nb + i)),
        ],
        out_shape=[
            jax.ShapeDtypeStruct((n, d), F32),
            jax.ShapeDtypeStruct((N_EXPERTS, n), F32),
        ],
        compiler_params=_cparams(2),
        name="oproj_ln_router",
    )(mixT, memT, x2d, wo, ln, wrT)


def _route_select_kernel(affT_ref, tri_ref, ones_ref, low_ref, posm_ref, off_ref, *, cap):
    a = affT_ref[0]
    bits = lax.bitcast_convert_type(a, jnp.int32)

    def count(mask):
        return jnp.sum(jnp.sum(mask.astype(F32), axis=1, keepdims=True), axis=0, keepdims=True)

    def bisect(i, t):
        cand = t | jnp.left_shift(jnp.int32(1), 30 - i)
        return jnp.where(count(bits >= cand) >= cap, cand, t)

    t = lax.fori_loop(0, 31, bisect, jnp.zeros((1, 1), jnp.int32))

    def excl_cumsum(mask):
        mb = mask.astype(BF16)
        incl = jnp.dot(mb, tri_ref[...], preferred_element_type=F32)
        tot = jnp.dot(mb, ones_ref[...], preferred_element_type=F32)
        before = jnp.dot(low_ref[...], tot.astype(BF16), preferred_element_type=F32)
        return before + incl - mask.astype(F32), before

    gt = bits > t
    eq = bits == t
    need = cap - count(gt)
    rank, _ = excl_cumsum(eq)
    sel = gt | (eq & (rank < need))
    pos, before = excl_cumsum(sel)
    posm_ref[0] = jnp.where(sel, pos, -1.0).astype(jnp.int32)
    off_ref[0] = before[:, 0:1].astype(jnp.int32)


def _route_select(affT3, cap):
    e, r, _ = affT3.shape
    tri = (jnp.arange(LANES)[:, None] <= jnp.arange(LANES)[None, :]).astype(BF16)
    ones = jnp.ones((LANES, LANES), BF16)
    low = (jnp.arange(r)[None, :] < jnp.arange(r)[:, None]).astype(BF16)
    posm, off = pl.pallas_call(
        functools.partial(_route_select_kernel, cap=cap),
        grid=(e,),
        in_specs=[pl.BlockSpec((1, r, LANES), lambda ei: (ei, 0, 0)),
                  pl.BlockSpec((LANES, LANES), lambda ei: (0, 0)),
                  pl.BlockSpec((LANES, LANES), lambda ei: (0, 0)),
                  pl.BlockSpec((r, r), lambda ei: (0, 0))],
        out_specs=[pl.BlockSpec((1, r, LANES), lambda ei: (ei, 0, 0)),
                   pl.BlockSpec((1, r, 1), lambda ei: (ei, 0, 0))],
        out_shape=[jax.ShapeDtypeStruct((e, r, LANES), jnp.int32),
                   jax.ShapeDtypeStruct((e, r, 1), jnp.int32)],
        compiler_params=_cparams(1),
        name="route_select",
    )(affT3, tri, ones, low)
    return posm, off.reshape(e, r)


COMPACT_ROWS = 16


def _route_compact_kernel(off_ref, posm_ref, affT_ref, idx_ref, gate_ref, acc_ref, *, nrows, ncb):
    e = pl.program_id(0)
    acc_ref[...] = jnp.zeros_like(acc_ref)
    slot = lax.broadcasted_iota(jnp.int32, (LANES, LANES), 0)
    sub = lax.broadcasted_iota(jnp.int32, (COMPACT_ROWS, LANES), 0)
    lane = lax.broadcasted_iota(jnp.int32, (COMPACT_ROWS, LANES), 1).astype(F32)

    def body(r, carry):
        cb = lax.shift_right_logical(off_ref[e, r], 7)
        rel = posm_ref[0, pl.ds(r, 1), :] - cb * LANES
        a = affT_ref[0, pl.ds(r, 1), :]
        hi = a.astype(BF16).astype(F32)
        mid = (a - hi).astype(BF16).astype(F32)
        lo = a - hi - mid
        vals = jnp.where(sub == 0, lane, jnp.where(sub == 1, jnp.asarray(r, F32), jnp.where(
            sub == 2, hi, jnp.where(sub == 3, mid, jnp.where(sub == 4, lo, 0.0))))).astype(BF16)
        for part in range(2):
            hit = ((rel - part * LANES) == slot).astype(BF16)
            acc_ref[cb + part] += lax.dot_general(vals, hit, _NT, preferred_element_type=F32)
        return carry

    lax.fori_loop(0, nrows, body, 0, unroll=8)
    acc = acc_ref[0:ncb]
    idx_ref[0] = (acc[:, 1:2, :] * LANES + acc[:, 0:1, :]).astype(jnp.int32)
    gate_ref[0] = acc[:, 2:3, :] + acc[:, 3:4, :] + acc[:, 4:5, :]


def _route_compact(off, posm, affT3, cap):
    e, r, _ = posm.shape
    ncb = cap // LANES
    idx, gate = pl.pallas_call(
        functools.partial(_route_compact_kernel, nrows=r, ncb=ncb),
        grid=(e,),
        in_specs=[pl.BlockSpec(memory_space=pltpu.SMEM),
                  pl.BlockSpec((1, r, LANES), lambda ei: (ei, 0, 0)),
                  pl.BlockSpec((1, r, LANES), lambda ei: (ei, 0, 0))],
        out_specs=[pl.BlockSpec((1, ncb, 1, LANES), lambda ei: (ei, 0, 0, 0)),
                   pl.BlockSpec((1, ncb, 1, LANES), lambda ei: (ei, 0, 0, 0))],
        out_shape=[jax.ShapeDtypeStruct((e, ncb, 1, LANES), jnp.int32),
                   jax.ShapeDtypeStruct((e, ncb, 1, LANES), F32)],
        scratch_shapes=[pltpu.VMEM((ncb + 2, COMPACT_ROWS, LANES), F32)],
        compiler_params=_cparams(1),
        name="route_compact",
    )(off, posm, affT3)
    return idx.reshape(e, cap), gate.reshape(e, cap, 1)


FFN_ROWS = 512


def _ffn_kernel(idx_ref, nxt_ref, x_hbm, wg_ref, wu_ref, wd_ref, gate_ref, y_ref,
                xf_ref, xb_ref, acc_ref, sem, *, nf, nblk):
    f = pl.program_id(2)
    blk = pl.program_id(0) * pl.num_programs(1) + pl.program_id(1)
    slot = blk % 2
    tc = xb_ref.shape[0]
    tcp = xf_ref.shape[1]
    per_step = tcp // nf
    chunk = min(FFN_ROWS, tc)

    def row_copy(ids, i, s):
        return pltpu.make_async_copy(x_hbm.at[pl.ds(ids[0, 0, i], 1)], xf_ref.at[s, pl.ds(i, 1)], sem.at[s])

    def all_rows(ids, s, fn):
        def one(i, c):
            fn(row_copy(ids, i, s))
            return c
        lax.fori_loop(0, tcp, one, 0, unroll=8)

    @pl.when(f == 0)
    def _():
        @pl.when(blk == 0)
        def _():
            all_rows(idx_ref, slot, lambda c: c.start())

        all_rows(idx_ref, slot, lambda c: c.wait())
        for c in range(tc // chunk):
            rows = slice(c * chunk, (c + 1) * chunk)
            xb_ref[rows, :] = xf_ref[slot, rows, :].astype(BF16)
        acc_ref[...] = jnp.zeros_like(acc_ref)

    wg = wg_ref[0, 0].astype(BF16)
    wu = wu_ref[0, 0].astype(BF16)
    wd = wd_ref[0, 0].astype(BF16)
    for c in range(tc // chunk):
        rows = slice(c * chunk, (c + 1) * chunk)
        x = xb_ref[rows, :]
        g = jnp.dot(x, wg, preferred_element_type=F32)
        u = jnp.dot(x, wu, preferred_element_type=F32)
        h = (g / (1.0 + jnp.exp(-g))) * u
        acc_ref[rows, :] += jnp.dot(h.astype(BF16), wd, preferred_element_type=F32)

    for j in range(per_step):
        row_copy(nxt_ref, f * per_step + j, 1 - slot).start()

    @pl.when(f == nf - 1)
    def _():
        y_ref[0] = (acc_ref[...] * gate_ref[0]).astype(y_ref.dtype)

        @pl.when(blk == nblk - 1)
        def _():
            all_rows(nxt_ref, 1 - slot, lambda c: c.wait())


def _expert_ffn(idx, x2d, w_gu, w_dn, layer, gates):
    e, c = idx.shape
    d = x2d.shape[1]
    fdim = w_dn.shape[2]
    tc = min(2048, c)
    ncb = c // tc
    nblk = e * ncb
    tf = 256
    nf = fdim // tf
    assert tc % min(FFN_ROWS, tc) == 0
    per_step = -(-tc // nf)
    per_step += -per_step % 8
    tcp = per_step * nf
    kern = functools.partial(_ffn_kernel, nf=nf, nblk=nblk)
    idx3 = jnp.pad(idx.reshape(nblk, 1, tc), ((0, 0), (0, 0), (0, tcp - tc)))
    return pl.pallas_call(
        kern,
        grid=(e, ncb, nf),
        in_specs=[
            pl.BlockSpec((1, 1, tcp), lambda ei, ci, fi: (ei * ncb + ci, 0, 0), memory_space=pltpu.SMEM),
            pl.BlockSpec((1, 1, tcp), lambda ei, ci, fi: (jnp.minimum(ei * ncb + ci + 1, nblk - 1), 0, 0),
                         memory_space=pltpu.SMEM),
            pl.BlockSpec(memory_space=pl.ANY),
            pl.BlockSpec((1, 1, d, tf), lambda ei, ci, fi: (layer, ei, 0, fi)),
            pl.BlockSpec((1, 1, d, tf), lambda ei, ci, fi: (layer, ei, 0, fi + nf)),
            pl.BlockSpec((1, 1, tf, d), lambda ei, ci, fi: (layer, ei, fi, 0)),
            pl.BlockSpec((1, tc, 1), lambda ei, ci, fi: (ei, ci, 0)),
        ],
        out_specs=pl.BlockSpec((1, tc, d), lambda ei, ci, fi: (ei, ci, 0)),
        out_shape=jax.ShapeDtypeStruct((e, c, d), BF16),
        scratch_shapes=[pltpu.VMEM((2, tcp, d), F32), pltpu.VMEM((tc, d), BF16), pltpu.VMEM((tc, d), F32),
                        pltpu.SemaphoreType.DMA((2,))],
        compiler_params=_cparams(3),
        name="expert_ffn",
    )(idx3, idx3, x2d, w_gu, w_gu, w_dn, gates)


COMBINE_ROWS = 8
BF16_ROWS = 16
WIN = LANES + BF16_ROWS
WIN_SHORT = 3 * BF16_ROWS


def _combine_kernel(off_ref, x_ref, posm_ref, y_hbm, ln_ref, o_ref, ybuf, ysh, sem, *, alpha, cap):
    t = pl.program_id(0)
    n_exp = posm_ref.shape[0]
    d = x_ref.shape[1]

    def plan(rr):
        starts = []
        short = None
        for e in range(n_exp):
            start = lax.shift_right_logical(off_ref[e, t * COMBINE_ROWS + rr], 4) * BF16_ROWS
            start = pl.multiple_of(jnp.minimum(start, cap - WIN), BF16_ROWS)
            fits = off_ref[e, t * COMBINE_ROWS + rr + 1] - start <= WIN_SHORT
            short = fits if short is None else short & fits
            starts.append(start)
        return starts, short

    def either(rr, par, fn):
        starts, short = plan(rr)
        for cond, nrows, buf in ((short, WIN_SHORT, ysh), (jnp.logical_not(short), WIN, ybuf)):
            @pl.when(cond)
            def _():
                copies = [pltpu.make_async_copy(y_hbm.at[e, pl.ds(starts[e], nrows)], buf.at[par, e], sem.at[par])
                          for e in range(n_exp)]
                fn(starts, nrows, buf, copies)

    def start_row(starts, nrows, buf, copies):
        for c in copies:
            c.start()

    def row(rr, par):
        @pl.when(rr + 1 < COMBINE_ROWS)
        def _():
            either(rr + 1, 1 - par, start_row)

        def finish(starts, nrows, buf, copies):
            for c in copies:
                c.wait()
            slot = lax.broadcasted_iota(jnp.int32, (nrows, LANES), 0)
            hit = jnp.concatenate(
                [((posm_ref[e, pl.ds(rr, 1), :] - starts[e]) == slot).astype(BF16) for e in range(n_exp)],
                axis=0)
            tot = lax.dot_general(hit, buf[par].reshape(n_exp * nrows, d), _TN, preferred_element_type=F32)
            rows = pl.ds(pl.multiple_of(rr * LANES, LANES), LANES)
            o_ref[rows, :] = _layer_norm(alpha * x_ref[rows, :] + tot, ln_ref[0:1, :], ln_ref[1:2, :])

        either(rr, par, finish)

    either(0, 0, start_row)

    def body(k, carry):
        row(2 * k, 0)
        row(2 * k + 1, 1)
        return carry

    lax.fori_loop(0, COMBINE_ROWS // 2, body, 0)


def _combine_ln(off, x2d, posm, y, ln, alpha, cap):
    n, d = x2d.shape
    e, r, _ = posm.shape
    rows = COMBINE_ROWS
    assert r % rows == 0 and rows % 2 == 0 and cap >= WIN
    tt = rows * LANES
    off = jnp.concatenate([off, jnp.full((e, 1), cap, jnp.int32)], axis=1)
    return pl.pallas_call(
        functools.partial(_combine_kernel, alpha=alpha, cap=cap),
        grid=(n // tt,),
        in_specs=[pl.BlockSpec(memory_space=pltpu.SMEM),
                  pl.BlockSpec((tt, d), lambda ti: (ti, 0)),
                  pl.BlockSpec((e, rows, LANES), lambda ti: (0, ti, 0)),
                  pl.BlockSpec(memory_space=pl.ANY),
                  pl.BlockSpec((2, d), lambda ti: (0, 0))],
        out_specs=pl.BlockSpec((tt, d), lambda ti: (ti, 0)),
        out_shape=jax.ShapeDtypeStruct((n, d), F32),
        scratch_shapes=[pltpu.VMEM((2, e, WIN, d), BF16), pltpu.VMEM((2, e, WIN_SHORT, d), BF16),
                        pltpu.SemaphoreType.DMA((2,))],
        compiler_params=_cparams(1),
        name="combine_ln",
    )(off, x2d, posm, y, ln)


def _rope_tables(seq):
    def tab(pos, dim):
        inv = ROPE_THETA ** (-jnp.arange(0, dim, 2, dtype=F32) / dim)
        ang = pos.astype(F32)[:, None] * inv[None, :]
        return jnp.cos(ang), jnp.sin(ang)

    c1, s1 = tab(jnp.arange(seq), HEAD_DIM)
    cos1 = jnp.concatenate([c1, c1], axis=-1)
    sin1 = jnp.concatenate([-s1, s1], axis=-1)
    cr, sr = tab(jnp.arange(seq) // GRID_W, HEAD_DIM // 2)
    cc, sc = tab(jnp.arange(seq) % GRID_W, HEAD_DIM // 2)
    cos2 = jnp.concatenate([cr, cr, cc, cc], axis=-1)
    sin2 = jnp.concatenate([-sr, sr, -sc, sc], axis=-1)
    rep = LANES // HEAD_DIM
    nat = lambda c, s: (jnp.tile(c, (1, rep)), jnp.tile(s, (1, rep)))
    return ((cos1.T, sin1.T), nat(cos1, sin1)), ((cos2.T, sin2.T), nat(cos2, sin2))


def _prep_params(p):
    bf = lambda w: w.astype(BF16)
    tr = lambda w: jnp.swapaxes(w, -1, -2).astype(BF16)
    a, bw, cw = p["a_w_in"], p["b_w_in"], p["c_w_in"]
    out = dict(p)
    out.update(
        a_wqT=tr(a[:, :, 0:MIX_W]), a_wk=bf(a[:, :, MIX_W:2 * MIX_W]),
        a_wvT=tr(a[:, :, 2 * MIX_W:3 * MIX_W]), a_wmT=tr(a[:, :, 3 * MIX_W:]),
        w_memk=bf(p["w_mem_kv"][:, :, 0:MEM_W]), w_memvT=tr(p["w_mem_kv"][:, :, MEM_W:]),
        w_o=bf(p["w_o"]), w_routerT=tr(p["w_router"]),
    )
    for name, w in (("b", bw), ("c", cw)):
        out[name + "_wqT"] = tr(w[:, :, 0:MIX_W])
        out[name + "_wk"] = bf(w[:, :, MIX_W:MIX_W + KV_W])
        out[name + "_wvT"] = tr(w[:, :, MIX_W + KV_W:MIX_W + 2 * KV_W])
        out[name + "_wmT"] = tr(w[:, :, MIX_W + 2 * KV_W:])
    return out


def _encode(x, mem, p):
    b, s, d = x.shape
    n = b * s
    m_len = mem.shape[1]
    depth = p["w_o"].shape[0]
    alpha = (2 * depth) ** 0.25
    (tabs1T, tabs1), (tabs2T, tabs2) = _rope_tables(s)
    bd = (jnp.arange(LANES)[:, None] // HEAD_DIM == jnp.arange(LANES)[None, :] // HEAD_DIM).astype(BF16)
    ones_g = jnp.ones((1, LANES), F32)
    ones_gT = jnp.ones((HEAD_DIM, LANES), F32)
    cap = EC_FACTOR * n // N_EXPERTS
    x2d = x.reshape(n, d)
    mem2d = mem.reshape(b * m_len, d)
    tq_full, tk_full = 512, 512

    for i in range(depth):
        kind, j = i % N_MIXERS, i // N_MIXERS
        common = dict(batch=b, seq=s)
        if kind == 0:
            qT, k4, vT5, qmT = _inproj(x2d, p["a_wqT"][j], p["a_wvT"][j], p["a_wmT"][j], p["a_wk"][j],
                                       tabs1T, tabs1, ones_gT, ones_g, bd, half=HEAD_DIM // 2,
                                       qk_norm=False, vchunk=tk_full, **common)
            lam_init = 0.8 - 0.6 * math.exp(-0.3 * i)
            tk = min(tk_full, s)
            k5 = k4.reshape(b, 2 * A_HEADS, s // tk, tk, HEAD_DIM)
            g_col = jnp.broadcast_to(p["a_subln"][j][:, None], (A_V_DIM, LANES))
            mixT = _diff_attention(qT, k5, vT5, p["a_lambda"][j], g_col, lam_init, 2 * tq_full)
        elif kind == 1:
            qT, k4, vT5, qmT = _inproj(x2d, p["b_wqT"][j], p["b_wvT"][j], p["b_wmT"][j], p["b_wk"][j],
                                       tabs1T, tabs1, ones_gT, ones_g, bd, half=HEAD_DIM // 2,
                                       qk_norm=False, vchunk=LANES, **common)
            k5 = k4.reshape(b, GQA_KV_HEADS, s // LANES, LANES, HEAD_DIM)
            mixT = _window_attention(qT, k5, vT5, p["b_sink"][j], GQA_GROUP, tq_full)
        else:
            gq = jnp.broadcast_to(p["c_qk_norm"][j][0][:, None], (HEAD_DIM, LANES))
            gk = jnp.tile(p["c_qk_norm"][j][1], LANES // HEAD_DIM)[None, :]
            qT, k4, vT5, qmT = _inproj(x2d, p["c_wqT"][j], p["c_wvT"][j], p["c_wmT"][j], p["c_wk"][j],
                                       tabs2T, tabs2, gq, gk, bd, half=HEAD_DIM // 4,
                                       qk_norm=True, vchunk=tk_full // 2, **common)
            tk = min(tk_full // 2, s)
            k5 = k4.reshape(b, GQA_KV_HEADS, s // tk, tk, HEAD_DIM)
            mixT = _gqa_attention(qT, k5, vT5, GQA_GROUP, 2 * tq_full)

        km, vmT = _memproj(mem2d, p["w_memk"][i], p["w_memvT"][i], b, m_len)
        memT = _gqa_attention(qmT, km.reshape(b, MEM_HEADS, 1, m_len, HEAD_DIM), vmT, 1, 2 * tq_full)

        x2d, affT = _oproj(mixT, memT, x2d, p["w_o"][i], p["ln_mix"][i], p["w_routerT"][i], alpha, b, s)

        affT3 = affT.reshape(N_EXPERTS, n // LANES, LANES)
        posm, off = _route_select(affT3, cap)
        idx, gates = _route_compact(off, posm, affT3, cap)
        y = _expert_ffn(idx, x2d, p["w_gate_up"], p["w_down"], i, gates)
        x2d = _combine_ln(off, x2d, posm, y, p["ln_ffn"][i], alpha, cap)
    return x2d.reshape(b, s, d)


def kernel(x_prompt, x_sample, mem_prompt, mem_sample, a_w_in, a_lambda, a_subln, b_w_in, b_sink, c_w_in, c_qk_norm, w_mem_kv, w_o, ln_mix, w_router, w_gate_up, w_down, ln_ffn):
    p = _prep_params(dict(
        a_w_in=a_w_in, a_lambda=a_lambda, a_subln=a_subln, b_w_in=b_w_in, b_sink=b_sink,
        c_w_in=c_w_in, c_qk_norm=c_qk_norm, w_mem_kv=w_mem_kv, w_o=w_o, ln_mix=ln_mix,
        w_router=w_router, w_gate_up=w_gate_up, w_down=w_down, ln_ffn=ln_ffn))
    return (_encode(x_prompt, mem_prompt, p), _encode(x_sample, mem_sample, p))
```

```python
import functools
import math

import jax
import jax.numpy as jnp
from jax import lax
from jax.experimental import pallas as pl
from jax.experimental.pallas import tpu as pltpu

F32 = jnp.float32
BF16 = jnp.bfloat16

HEAD_DIM = 64
GQA_KV_HEADS = 4
GQA_GROUP = 3
A_HEADS = 6
A_V_DIM = 2 * HEAD_DIM
MEM_HEADS = 4
MIX_W = 768
MEM_W = MEM_HEADS * HEAD_DIM
KV_W = GQA_KV_HEADS * HEAD_DIM
N_MIXERS = 3
WINDOW = 128
GRID_W = 64
ROPE_THETA = 10000.0
N_EXPERTS = 16
EC_FACTOR = 2
LN_EPS = 1e-5
RMS_EPS = 1e-6
LOG2E = 1.4426950408889634
QK_SCALE = HEAD_DIM ** -0.5 * LOG2E
LANES = 128
ONES_ROWS = 16
ATTN_UNROLL = 4
VMEM_LIMIT = 52 * 1024 * 1024

_NT = (((1,), (1,)), ((), ()))
_TN = (((0,), (0,)), ((), ()))


def _cparams(grid_rank):
    return pltpu.CompilerParams(dimension_semantics=("arbitrary",) * grid_rank,
                                vmem_limit_bytes=VMEM_LIMIT)


def _layer_norm(h, g, b):
    mu = jnp.mean(h, axis=-1, keepdims=True)
    d = h - mu
    var = jnp.mean(d * d, axis=-1, keepdims=True)
    return d * lax.rsqrt(var + LN_EPS) * g + b


def _lane_tile(a, width):
    return jnp.concatenate([a] * (width // a.shape[1]), axis=1)


def _inproj_kernel(x_ref, wqT_ref, wvT_ref, wmT_ref, wk_ref, cosT_ref, sinT_ref, cos_ref, sin_ref,
                   gT_ref, g_ref, bd_ref, qT_ref, k_ref, vT_ref, qmT_ref, *, half, qk_norm, vchunk):
    xb = x_ref[...].astype(BF16)
    tm = xb.shape[0]
    cosT = cosT_ref[...]
    sinT = sinT_ref[...]
    nparts = HEAD_DIM // half

    yqT = lax.dot_general(wqT_ref[...], xb, _NT, preferred_element_type=F32)
    for h in range(yqT.shape[0] // HEAD_DIM):
        xh = yqT[h * HEAD_DIM:(h + 1) * HEAD_DIM]
        if qk_norm:
            ms = jnp.mean(xh * xh, axis=0, keepdims=True)
            xh = xh * lax.rsqrt(ms + RMS_EPS) * _lane_tile(gT_ref[...], tm)
        rot = jnp.concatenate([xh[(p ^ 1) * half:((p ^ 1) + 1) * half] for p in range(nparts)], axis=0)
        qT_ref[0, h * HEAD_DIM:(h + 1) * HEAD_DIM, :] = ((xh * cosT + rot * sinT) * QK_SCALE).astype(BF16)

    yvT = lax.dot_general(wvT_ref[...], xb, _NT, preferred_element_type=F32).astype(BF16)
    for c in range(tm // vchunk):
        vT_ref[0, c] = yvT[:, c * vchunk:(c + 1) * vchunk]
    ymT = lax.dot_general(wmT_ref[...], xb, _NT, preferred_element_type=F32)
    qmT_ref[0] = (ymT * QK_SCALE).astype(BF16)

    lane = lax.broadcasted_iota(jnp.int32, (tm, LANES), 1)
    first = (lane % (2 * half)) < half
    cos = cos_ref[...]
    sin = sin_ref[...]
    yk = jnp.dot(xb, wk_ref[...], preferred_element_type=F32)
    for c in range(yk.shape[1] // LANES):
        ch = yk[:, c * LANES:(c + 1) * LANES]
        if qk_norm:
            sq = ch * ch
            hi = sq.astype(BF16)
            lo = (sq - hi.astype(F32)).astype(BF16)
            ss = (jnp.dot(hi, bd_ref[...], preferred_element_type=F32)
                  + jnp.dot(lo, bd_ref[...], preferred_element_type=F32))
            ch = ch * lax.rsqrt(ss * (1.0 / HEAD_DIM) + RMS_EPS) * g_ref[...]
        rot = jnp.where(first, pltpu.roll(ch, LANES - half, 1), pltpu.roll(ch, half, 1))
        ch = (ch * cos + rot * sin).astype(BF16)
        k_ref[0, 2 * c] = ch[:, 0:HEAD_DIM]
        k_ref[0, 2 * c + 1] = ch[:, HEAD_DIM:LANES]


def _inproj(x2d, wqT, wvT, wmT, wk, tabsT, tabs, gT, g, bd, *, batch, seq, half, qk_norm, vchunk):
    n, d = x2d.shape
    tm = min(512, seq)
    vchunk = min(vchunk, tm)
    nb = seq // tm
    qw, vw, kw = wqT.shape[0], wvT.shape[0], wk.shape[1]
    kern = functools.partial(_inproj_kernel, half=half, qk_norm=qk_norm, vchunk=vchunk)
    const = lambda b, i: (0, 0)
    return pl.pallas_call(
        kern,
        grid=(batch, nb),
        in_specs=[
            pl.BlockSpec((tm, d), lambda b, i: (b * nb + i, 0)),
            pl.BlockSpec(wqT.shape, const),
            pl.BlockSpec(wvT.shape, const),
            pl.BlockSpec(wmT.shape, const),
            pl.BlockSpec(wk.shape, const),
            pl.BlockSpec((HEAD_DIM, tm), lambda b, i: (0, i)),
            pl.BlockSpec((HEAD_DIM, tm), lambda b, i: (0, i)),
            pl.BlockSpec((tm, LANES), lambda b, i: (i, 0)),
            pl.BlockSpec((tm, LANES), lambda b, i: (i, 0)),
            pl.BlockSpec((HEAD_DIM, LANES), const),
            pl.BlockSpec((1, LANES), const),
            pl.BlockSpec((LANES, LANES), const),
        ],
        out_specs=[
            pl.BlockSpec((1, qw, tm), lambda b, i: (b, 0, i)),
            pl.BlockSpec((1, kw // HEAD_DIM, tm, HEAD_DIM), lambda b, i: (b, 0, i, 0)),
            pl.BlockSpec((1, tm // vchunk, vw, vchunk), lambda b, i: (b, i, 0, 0)),
            pl.BlockSpec((1, MEM_W, tm), lambda b, i: (b, 0, i)),
        ],
        out_shape=[
            jax.ShapeDtypeStruct((batch, qw, seq), BF16),
            jax.ShapeDtypeStruct((batch, kw // HEAD_DIM, seq, HEAD_DIM), BF16),
            jax.ShapeDtypeStruct((batch, seq // vchunk, vw, vchunk), BF16),
            jax.ShapeDtypeStruct((batch, MEM_W, seq), BF16),
        ],
        compiler_params=_cparams(2),
        name="inproj",
    )(x2d, wqT, wvT, wmT, wk, *tabsT, *tabs, gT, g, bd)


def _memproj_kernel(x_ref, wk_ref, wvT_ref, k_ref, vT_ref):
    xb = x_ref[...].astype(BF16)
    yk = jnp.dot(xb, wk_ref[...], preferred_element_type=F32).astype(BF16)
    for h in range(MEM_HEADS):
        k_ref[0, h] = yk[:, h * HEAD_DIM:(h + 1) * HEAD_DIM]
    vT_ref[0, 0] = lax.dot_general(wvT_ref[...], xb, _NT, preferred_element_type=F32).astype(BF16)


def _memproj(mem2d, wk, wvT, batch, m_len):
    d = mem2d.shape[1]
    return pl.pallas_call(
        _memproj_kernel,
        grid=(batch,),
        in_specs=[pl.BlockSpec((m_len, d), lambda b: (b, 0)),
                  pl.BlockSpec(wk.shape, lambda b: (0, 0)),
                  pl.BlockSpec(wvT.shape, lambda b: (0, 0))],
        out_specs=[pl.BlockSpec((1, MEM_HEADS, m_len, HEAD_DIM), lambda b: (b, 0, 0, 0)),
                   pl.BlockSpec((1, 1, MEM_W, m_len), lambda b: (b, 0, 0, 0))],
        out_shape=[jax.ShapeDtypeStruct((batch, MEM_HEADS, m_len, HEAD_DIM), BF16),
                   jax.ShapeDtypeStruct((batch, 1, MEM_W, m_len), BF16)],
        compiler_params=_cparams(1),
        name="memproj",
    )(mem2d, wk, wvT)


def _ones_rows(tk):
    return (lax.broadcasted_iota(jnp.int32, (ONES_ROWS, tk), 0) == 0).astype(BF16)


def _flash_step(sT, va, m, acc):
    m_new = jnp.maximum(m, jnp.max(sT, axis=0, keepdims=True))
    a = jnp.exp2(m - m_new)
    pT = jnp.exp2(sT - m_new).astype(BF16)
    return m_new, a * acc + jnp.dot(va, pT, preferred_element_type=F32)


def _gqa_kernel(qT_ref, k_ref, vT_ref, o_ref, *, group, nk, unroll):
    tk = k_ref.shape[3]
    qT = jnp.concatenate([qT_ref[0, g * HEAD_DIM:(g + 1) * HEAD_DIM, :] for g in range(group)], axis=1)
    nq = qT.shape[1]
    tq = nq // group
    ones = _ones_rows(tk)

    def body(j, carry):
        sT = jnp.dot(k_ref[0, 0, j], qT, preferred_element_type=F32)
        va = jnp.concatenate([vT_ref[0, j], ones], axis=0)
        return _flash_step(sT, va, *carry)

    init = (jnp.full((1, nq), -jnp.inf, F32), jnp.zeros((HEAD_DIM + ONES_ROWS, nq), F32))
    _, acc = lax.fori_loop(0, nk, body, init, unroll=unroll)
    o = acc[0:HEAD_DIM] / acc[HEAD_DIM:HEAD_DIM + 1]
    for g in range(group):
        o_ref[0, g * HEAD_DIM:(g + 1) * HEAD_DIM, :] = o[:, g * tq:(g + 1) * tq].astype(o_ref.dtype)


def _gqa_attention(qT, k5, vT5, group, tq):
    b, qrows, s = qT.shape
    _, kvh, nk, tk, _ = k5.shape
    tq = min(tq, s)
    kern = functools.partial(_gqa_kernel, group=group, nk=nk, unroll=math.gcd(nk, 2 * ATTN_UNROLL))
    return pl.pallas_call(
        kern,
        grid=(b, kvh, s // tq),
        in_specs=[
            pl.BlockSpec((1, group * HEAD_DIM, tq), lambda bi, h, i: (bi, h, i)),
            pl.BlockSpec((1, 1, nk, tk, HEAD_DIM), lambda bi, h, i: (bi, h, 0, 0, 0)),
            pl.BlockSpec((1, nk, HEAD_DIM, tk), lambda bi, h, i: (bi, 0, h, 0)),
        ],
        out_specs=pl.BlockSpec((1, group * HEAD_DIM, tq), lambda bi, h, i: (bi, h, i)),
        out_shape=jax.ShapeDtypeStruct(qT.shape, BF16),
        compiler_params=_cparams(3),
        name="gqa_attention",
    )(qT, k5, vT5)


def _diff_kernel(lam_ref, g_ref, qT_ref, k_ref, vT_ref, o_ref, *, nk, unroll, lam_init):
    tk = k_ref.shape[3]
    tq = qT_ref.shape[2]
    lp = lam_ref[...]
    lam = (jnp.exp(jnp.sum(lp[0:1] * lp[1:2], axis=-1, keepdims=True))
           - jnp.exp(jnp.sum(lp[2:3] * lp[3:4], axis=-1, keepdims=True)) + lam_init)
    q0T = qT_ref[0, 0:HEAD_DIM, :]
    q1T = qT_ref[0, HEAD_DIM:2 * HEAD_DIM, :]
    ones = _ones_rows(tk)

    def body(j, carry):
        sT = jnp.concatenate([jnp.dot(k_ref[0, 0, j], q0T, preferred_element_type=F32),
                              jnp.dot(k_ref[0, 1, j], q1T, preferred_element_type=F32)], axis=1)
        va = jnp.concatenate([vT_ref[0, j], ones], axis=0)
        return _flash_step(sT, va, *carry)

    init = (jnp.full((1, 2 * tq), -jnp.inf, F32), jnp.zeros((A_V_DIM + ONES_ROWS, 2 * tq), F32))
    _, acc = lax.fori_loop(0, nk, body, init, unroll=unroll)
    o = acc[0:A_V_DIM] / acc[A_V_DIM:A_V_DIM + 1]
    o = o[:, 0:tq] - lam * o[:, tq:2 * tq]
    ms = jnp.mean(o * o, axis=0, keepdims=True)
    o = o * lax.rsqrt(ms + RMS_EPS) * _lane_tile(g_ref[...], tq) * (1.0 - lam_init)
    o_ref[0] = o.astype(o_ref.dtype)


def _diff_attention(qT, k5, vT5, lam_p, g_col, lam_init, tq):
    b, qrows, s = qT.shape
    _, _, nk, tk, _ = k5.shape
    tq = min(tq, s)
    kern = functools.partial(_diff_kernel, nk=nk, unroll=math.gcd(nk, 2 * ATTN_UNROLL), lam_init=lam_init)
    return pl.pallas_call(
        kern,
        grid=(b, A_HEADS, s // tq),
        in_specs=[
            pl.BlockSpec((4, HEAD_DIM), lambda bi, h, i: (0, 0)),
            pl.BlockSpec((A_V_DIM, LANES), lambda bi, h, i: (0, 0)),
            pl.BlockSpec((1, 2 * HEAD_DIM, tq), lambda bi, h, i: (bi, h, i)),
            pl.BlockSpec((1, 2, nk, tk, HEAD_DIM), lambda bi, h, i: (bi, h, 0, 0, 0)),
            pl.BlockSpec((1, nk, A_V_DIM, tk), lambda bi, h, i: (bi, 0, h, 0)),
        ],
        out_specs=pl.BlockSpec((1, A_V_DIM, tq), lambda bi, h, i: (bi, h, i)),
        out_shape=jax.ShapeDtypeStruct(qT.shape, BF16),
        compiler_params=_cparams(3),
        name="diff_attention",
    )(lam_p, g_col, qT, k5, vT5)


def _window_kernel(sink_ref, qT_ref, k_ref, vT_ref, o_ref, *, group, nwc, nchunks):
    h = pl.program_id(1)
    tq = qT_ref.shape[2]
    nq = group * tq
    q0 = pl.program_id(2) * tq
    c0 = jnp.clip(q0 // LANES - WINDOW // LANES, 0, nchunks - nwc)
    qT = jnp.concatenate([qT_ref[0, g * HEAD_DIM:(g + 1) * HEAD_DIM, :] for g in range(group)], axis=1)
    qpos = q0 + lax.broadcasted_iota(jnp.int32, (LANES, nq), 1) % tq
    krow = lax.broadcasted_iota(jnp.int32, (LANES, nq), 0)
    sink = jnp.concatenate(
        [jnp.full((1, tq), sink_ref[h * group + g] * LOG2E, F32) for g in range(group)], axis=1)
    ones = _ones_rows(LANES)
    scores = []
    m = sink
    for c in range(nwc):
        sT = jnp.dot(k_ref[0, 0, c0 + c], qT, preferred_element_type=F32)
        kpos = (c0 + c) * LANES + krow
        sT = jnp.where(jnp.abs(kpos - qpos) <= WINDOW, sT, -jnp.inf)
        scores.append(sT)
        m = jnp.maximum(m, jnp.max(sT, axis=0, keepdims=True))
    acc = jnp.zeros((HEAD_DIM + ONES_ROWS, nq), F32)
    for c in range(nwc):
        va = jnp.concatenate([vT_ref[0, c0 + c], ones], axis=0)
        acc = acc + jnp.dot(va, jnp.exp2(scores[c] - m).astype(BF16), preferred_element_type=F32)
    o = acc[0:HEAD_DIM] / (acc[HEAD_DIM:HEAD_DIM + 1] + jnp.exp2(sink - m))
    for g in range(group):
        o_ref[0, g * HEAD_DIM:(g + 1) * HEAD_DIM, :] = o[:, g * tq:(g + 1) * tq].astype(o_ref.dtype)


def _window_attention(qT, k5, vT5, sink, group, tq):
    b, qrows, s = qT.shape
    _, kvh, nchunks, tk, _ = k5.shape
    tq = min(tq, s)
    nwc = min(nchunks, tq // LANES + 2 * (WINDOW // LANES))
    kern = functools.partial(_window_kernel, group=group, nwc=nwc, nchunks=nchunks)
    return pl.pallas_call(
        kern,
        grid=(b, kvh, s // tq),
        in_specs=[
            pl.BlockSpec(memory_space=pltpu.SMEM),
            pl.BlockSpec((1, group * HEAD_DIM, tq), lambda bi, h, i: (bi, h, i)),
            pl.BlockSpec((1, 1, nchunks, tk, HEAD_DIM), lambda bi, h, i: (bi, h, 0, 0, 0)),
            pl.BlockSpec((1, nchunks, HEAD_DIM, tk), lambda bi, h, i: (bi, 0, h, 0)),
        ],
        out_specs=pl.BlockSpec((1, group * HEAD_DIM, tq), lambda bi, h, i: (bi, h, i)),
        out_shape=jax.ShapeDtypeStruct(qT.shape, BF16),
        compiler_params=_cparams(3),
        name="window_attention",
    )(sink, qT, k5, vT5)


def _oproj_kernel(mixT_ref, memT_ref, x_ref, wo_ref, ln_ref, wrT_ref, xo_ref, affT_ref, *, alpha):
    sub = (lax.dot_general(mixT_ref[0], wo_ref[0:MIX_W, :], _TN, preferred_element_type=F32)
           + lax.dot_general(memT_ref[0], wo_ref[MIX_W:MIX_W + MEM_W, :], _TN,
                             preferred_element_type=F32))
    y = _layer_norm(alpha * x_ref[...] + sub, ln_ref[0:1, :], ln_ref[1:2, :])
    xo_ref[...] = y
    logitsT = lax.dot_general(wrT_ref[...], y.astype(BF16), _NT, preferred_element_type=F32)
    e = jnp.exp(logitsT - jnp.max(logitsT, axis=0, keepdims=True))
    affT_ref[...] = e / jnp.sum(e, axis=0, keepdims=True)


def _oproj(mixT, memT, x2d, wo, ln, wrT, alpha, batch, seq):
    n, d = x2d.shape
    tm = min(512, seq)
    nb = seq // tm
    kern = functools.partial(_oproj_kernel, alpha=alpha)
    const = lambda b, i: (0, 0)
    row = lambda b, i: (b * nb + i, 0)
    return pl.pallas_call(
        kern,
        grid=(batch, nb),
        in_specs=[
            pl.BlockSpec((1, MIX_W, tm), lambda b, i: (b, 0, i)),
            pl.BlockSpec((1, MEM_W, tm), lambda b, i: (b, 0, i)),
            pl.BlockSpec((tm, d), row),
            pl.BlockSpec(wo.shape, const),
            pl.BlockSpec((2, d), const),
            pl.BlockSpec(wrT.shape, const),
        ],
        out_specs=[
            pl.BlockSpec((tm, d), row),
            pl.BlockSpec((N_EXPERTS, tm), lambda b, i: (0, b * nb + i)),
        ],
        out_shape=[
            jax.ShapeDtypeStruct((n, d), F32),
            jax.ShapeDtypeStruct((N_EXPERTS, n), F32),
        ],
        compiler_params=_cparams(2),
        name="oproj_ln_router",
    )(mixT, memT, x2d, wo, ln, wrT)


def _route_select_kernel(affT_ref, tri_ref, ones_ref, low_ref, posm_ref, off_ref, *, cap):
    a = affT_ref[0]
    bits = lax.bitcast_convert_type(a, jnp.int32)

    def count(mask):
        return jnp.sum(jnp.sum(mask.astype(F32), axis=1, keepdims=True), axis=0, keepdims=True)

    def bisect(i, t):
        cand = t | jnp.left_shift(jnp.int32(1), 30 - i)
        return jnp.where(count(bits >= cand) >= cap, cand, t)

    t = lax.fori_loop(0, 31, bisect, jnp.zeros((1, 1), jnp.int32))

    def excl_cumsum(mask):
        mb = mask.astype(BF16)
        incl = jnp.dot(mb, tri_ref[...], preferred_element_type=F32)
        tot = jnp.dot(mb, ones_ref[...], preferred_element_type=F32)
        before = jnp.dot(low_ref[...], tot.astype(BF16), preferred_element_type=F32)
        return before + incl - mask.astype(F32), before

    gt = bits > t
    eq = bits == t
    need = cap - count(gt)
    rank, _ = excl_cumsum(eq)
    sel = gt | (eq & (rank < need))
    pos, before = excl_cumsum(sel)
    posm_ref[0] = jnp.where(sel, pos, -1.0).astype(jnp.int32)
    off_ref[0] = before[:, 0:1].astype(jnp.int32)


def _route_select(affT3, cap):
    e, r, _ = affT3.shape
    tri = (jnp.arange(LANES)[:, None] <= jnp.arange(LANES)[None, :]).astype(BF16)
    ones = jnp.ones((LANES, LANES), BF16)
    low = (jnp.arange(r)[None, :] < jnp.arange(r)[:, None]).astype(BF16)
    posm, off = pl.pallas_call(
        functools.partial(_route_select_kernel, cap=cap),
        grid=(e,),
        in_specs=[pl.BlockSpec((1, r, LANES), lambda ei: (ei, 0, 0)),
                  pl.BlockSpec((LANES, LANES), lambda ei: (0, 0)),
                  pl.BlockSpec((LANES, LANES), lambda ei: (0, 0)),
                  pl.BlockSpec((r, r), lambda ei: (0, 0))],
        out_specs=[pl.BlockSpec((1, r, LANES), lambda ei: (ei, 0, 0)),
                   pl.BlockSpec((1, r, 1), lambda ei: (ei, 0, 0))],
        out_shape=[jax.ShapeDtypeStruct((e, r, LANES), jnp.int32),
                   jax.ShapeDtypeStruct((e, r, 1), jnp.int32)],
        compiler_params=_cparams(1),
        name="route_select",
    )(affT3, tri, ones, low)
    return posm, off.reshape(e, r)


COMPACT_ROWS = 16


def _route_compact_kernel(off_ref, posm_ref, affT_ref, idx_ref, gate_ref, acc_ref, *, nrows, ncb):
    e = pl.program_id(0)
    acc_ref[...] = jnp.zeros_like(acc_ref)
    slot = lax.broadcasted_iota(jnp.int32, (LANES, LANES), 0)
    sub = lax.broadcasted_iota(jnp.int32, (COMPACT_ROWS, LANES), 0)
    lane = lax.broadcasted_iota(jnp.int32, (COMPACT_ROWS, LANES), 1).astype(F32)

    def body(r, carry):
        cb = lax.shift_right_logical(off_ref[e, r], 7)
        rel = posm_ref[0, pl.ds(r, 1), :] - cb * LANES
        a = affT_ref[0, pl.ds(r, 1), :]
        hi = a.astype(BF16).astype(F32)
        mid = (a - hi).astype(BF16).astype(F32)
        lo = a - hi - mid
        vals = jnp.where(sub == 0, lane, jnp.where(sub == 1, jnp.asarray(r, F32), jnp.where(
            sub == 2, hi, jnp.where(sub == 3, mid, jnp.where(sub == 4, lo, 0.0))))).astype(BF16)
        for part in range(2):
            hit = ((rel - part * LANES) == slot).astype(BF16)
            acc_ref[cb + part] += lax.dot_general(vals, hit, _NT, preferred_element_type=F32)
        return carry

    lax.fori_loop(0, nrows, body, 0, unroll=16)
    acc = acc_ref[0:ncb]
    idx_ref[0] = (acc[:, 1:2, :] * LANES + acc[:, 0:1, :]).astype(jnp.int32)
    gate_ref[0] = acc[:, 2:3, :] + acc[:, 3:4, :] + acc[:, 4:5, :]


def _route_compact(off, posm, affT3, cap):
    e, r, _ = posm.shape
    ncb = cap // LANES
    idx, gate = pl.pallas_call(
        functools.partial(_route_compact_kernel, nrows=r, ncb=ncb),
        grid=(e,),
        in_specs=[pl.BlockSpec(memory_space=pltpu.SMEM),
                  pl.BlockSpec((1, r, LANES), lambda ei: (ei, 0, 0)),
                  pl.BlockSpec((1, r, LANES), lambda ei: (ei, 0, 0))],
        out_specs=[pl.BlockSpec((1, ncb, 1, LANES), lambda ei: (ei, 0, 0, 0)),
                   pl.BlockSpec((1, ncb, 1, LANES), lambda ei: (ei, 0, 0, 0))],
        out_shape=[jax.ShapeDtypeStruct((e, ncb, 1, LANES), jnp.int32),
                   jax.ShapeDtypeStruct((e, ncb, 1, LANES), F32)],
        scratch_shapes=[pltpu.VMEM((ncb + 2, COMPACT_ROWS, LANES), F32)],
        compiler_params=_cparams(1),
        name="route_compact",
    )(off, posm, affT3)
    return idx.reshape(e, cap), gate.reshape(e, cap, 1)


FFN_ROWS = 512


def _ffn_kernel(idx_ref, nxt_ref, x_hbm, wg_ref, wu_ref, wd_ref, gate_ref, y_ref,
                xf_ref, xb_ref, acc_ref, sem, *, nf, nblk):
    f = pl.program_id(2)
    blk = pl.program_id(0) * pl.num_programs(1) + pl.program_id(1)
    slot = blk % 2
    tc = xb_ref.shape[0]
    tcp = xf_ref.shape[1]
    per_step = tcp // nf
    chunk = min(FFN_ROWS, tc)

    def row_copy(ids, i, s):
        return pltpu.make_async_copy(x_hbm.at[pl.ds(ids[0, 0, i], 1)], xf_ref.at[s, pl.ds(i, 1)], sem.at[s])

    def all_rows(ids, s, fn):
        def one(i, c):
            fn(row_copy(ids, i, s))
            return c
        lax.fori_loop(0, tcp, one, 0, unroll=8)

    @pl.when(f == 0)
    def _():
        @pl.when(blk == 0)
        def _():
            all_rows(idx_ref, slot, lambda c: c.start())

        all_rows(idx_ref, slot, lambda c: c.wait())
        for c in range(tc // chunk):
            rows = slice(c * chunk, (c + 1) * chunk)
            xb_ref[rows, :] = xf_ref[slot, rows, :].astype(BF16)
        acc_ref[...] = jnp.zeros_like(acc_ref)

    wg = wg_ref[0, 0].astype(BF16)
    wu = wu_ref[0, 0].astype(BF16)
    wd = wd_ref[0, 0].astype(BF16)
    for c in range(tc // chunk):
        rows = slice(c * chunk, (c + 1) * chunk)
        x = xb_ref[rows, :]
        g = jnp.dot(x, wg, preferred_element_type=F32)
        u = jnp.dot(x, wu, preferred_element_type=F32)
        h = (g / (1.0 + jnp.exp(-g))) * u
        acc_ref[rows, :] += jnp.dot(h.astype(BF16), wd, preferred_element_type=F32)

    for j in range(per_step):
        row_copy(nxt_ref, f * per_step + j, 1 - slot).start()

    @pl.when(f == nf - 1)
    def _():
        y_ref[0] = (acc_ref[...] * gate_ref[0]).astype(y_ref.dtype)

        @pl.when(blk == nblk - 1)
        def _():
            all_rows(nxt_ref, 1 - slot, lambda c: c.wait())


def _expert_ffn(idx, x2d, w_gu, w_dn, layer, gates):
    e, c = idx.shape
    d = x2d.shape[1]
    fdim = w_dn.shape[2]
    tc = min(2048, c)
    ncb = c // tc
    nblk = e * ncb
    tf = 256
    nf = fdim // tf
    assert tc % min(FFN_ROWS, tc) == 0
    per_step = -(-tc // nf)
    per_step += -per_step % 8
    tcp = per_step * nf
    kern = functools.partial(_ffn_kernel, nf=nf, nblk=nblk)
    idx3 = jnp.pad(idx.reshape(nblk, 1, tc), ((0, 0), (0, 0), (0, tcp - tc)))
    return pl.pallas_call(
        kern,
        grid=(e, ncb, nf),
        in_specs=[
            pl.BlockSpec((1, 1, tcp), lambda ei, ci, fi: (ei * ncb + ci, 0, 0), memory_space=pltpu.SMEM),
            pl.BlockSpec((1, 1, tcp), lambda ei, ci, fi: (jnp.minimum(ei * ncb + ci + 1, nblk - 1), 0, 0),
                         memory_space=pltpu.SMEM),
            pl.BlockSpec(memory_space=pl.ANY),
            pl.BlockSpec((1, 1, d, tf), lambda ei, ci, fi: (layer, ei, 0, fi)),
            pl.BlockSpec((1, 1, d, tf), lambda ei, ci, fi: (layer, ei, 0, fi + nf)),
            pl.BlockSpec((1, 1, tf, d), lambda ei, ci, fi: (layer, ei, fi, 0)),
            pl.BlockSpec((1, tc, 1), lambda ei, ci, fi: (ei, ci, 0)),
        ],
        out_specs=pl.BlockSpec((1, tc, d), lambda ei, ci, fi: (ei, ci, 0)),
        out_shape=jax.ShapeDtypeStruct((e, c, d), BF16),
        scratch_shapes=[pltpu.VMEM((2, tcp, d), F32), pltpu.VMEM((tc, d), BF16), pltpu.VMEM((tc, d), F32),
                        pltpu.SemaphoreType.DMA((2,))],
        compiler_params=_cparams(3),
        name="expert_ffn",
    )(idx3, idx3, x2d, w_gu, w_gu, w_dn, gates)


COMBINE_ROWS = 8
BF16_ROWS = 16
WIN = LANES + BF16_ROWS
WIN_SHORT = 3 * BF16_ROWS


def _combine_kernel(off_ref, x_ref, posm_ref, y_hbm, ln_ref, o_ref, ybuf, ysh, sem, *, alpha, cap):
    t = pl.program_id(0)
    n_exp = posm_ref.shape[0]
    d = x_ref.shape[1]

    def plan(rr):
        starts = []
        short = None
        for e in range(n_exp):
            start = lax.shift_right_logical(off_ref[e, t * COMBINE_ROWS + rr], 4) * BF16_ROWS
            start = pl.multiple_of(jnp.minimum(start, cap - WIN), BF16_ROWS)
            fits = off_ref[e, t * COMBINE_ROWS + rr + 1] - start <= WIN_SHORT
            short = fits if short is None else short & fits
            starts.append(start)
        return starts, short

    def either(rr, par, fn):
        starts, short = plan(rr)
        for cond, nrows, buf in ((short, WIN_SHORT, ysh), (jnp.logical_not(short), WIN, ybuf)):
            @pl.when(cond)
            def _():
                copies = [pltpu.make_async_copy(y_hbm.at[e, pl.ds(starts[e], nrows)], buf.at[par, e], sem.at[par])
                          for e in range(n_exp)]
                fn(starts, nrows, buf, copies)

    def start_row(starts, nrows, buf, copies):
        for c in copies:
            c.start()

    def row(rr, par):
        @pl.when(rr + 1 < COMBINE_ROWS)
        def _():
            either(rr + 1, 1 - par, start_row)

        def finish(starts, nrows, buf, copies):
            for c in copies:
                c.wait()
            slot = lax.broadcasted_iota(jnp.int32, (nrows, LANES), 0)
            hit = jnp.concatenate(
                [((posm_ref[e, pl.ds(rr, 1), :] - starts[e]) == slot).astype(BF16) for e in range(n_exp)],
                axis=0)
            tot = lax.dot_general(hit, buf[par].reshape(n_exp * nrows, d), _TN, preferred_element_type=F32)
            rows = pl.ds(pl.multiple_of(rr * LANES, LANES), LANES)
            o_ref[rows, :] = _layer_norm(alpha * x_ref[rows, :] + tot, ln_ref[0:1, :], ln_ref[1:2, :])

        either(rr, par, finish)

    either(0, 0, start_row)

    def body(k, carry):
        row(2 * k, 0)
        row(2 * k + 1, 1)
        return carry

    lax.fori_loop(0, COMBINE_ROWS // 2, body, 0)


def _combine_ln(off, x2d, posm, y, ln, alpha, cap):
    n, d = x2d.shape
    e, r, _ = posm.shape
    rows = COMBINE_ROWS
    assert r % rows == 0 and rows % 2 == 0 and cap >= WIN
    tt = rows * LANES
    off = jnp.concatenate([off, jnp.full((e, 1), cap, jnp.int32)], axis=1)
    return pl.pallas_call(
        functools.partial(_combine_kernel, alpha=alpha, cap=cap),
        grid=(n // tt,),
        in_specs=[pl.BlockSpec(memory_space=pltpu.SMEM),
                  pl.BlockSpec((tt, d), lambda ti: (ti, 0)),
                  pl.BlockSpec((e, rows, LANES), lambda ti: (0, ti, 0)),
                  pl.BlockSpec(memory_space=pl.ANY),
                  pl.BlockSpec((2, d), lambda ti: (0, 0))],
        out_specs=pl.BlockSpec((tt, d), lambda ti: (ti, 0)),
        out_shape=jax.ShapeDtypeStruct((n, d), F32),
        scratch_shapes=[pltpu.VMEM((2, e, WIN, d), BF16), pltpu.VMEM((2, e, WIN_SHORT, d), BF16),
                        pltpu.SemaphoreType.DMA((2,))],
        compiler_params=_cparams(1),
        name="combine_ln",
    )(off, x2d, posm, y, ln)


def _rope_tables(seq):
    def tab(pos, dim):
        inv = ROPE_THETA ** (-jnp.arange(0, dim, 2, dtype=F32) / dim)
        ang = pos.astype(F32)[:, None] * inv[None, :]
        return jnp.cos(ang), jnp.sin(ang)

    c1, s1 = tab(jnp.arange(seq), HEAD_DIM)
    cos1 = jnp.concatenate([c1, c1], axis=-1)
    sin1 = jnp.concatenate([-s1, s1], axis=-1)
    cr, sr = tab(jnp.arange(seq) // GRID_W, HEAD_DIM // 2)
    cc, sc = tab(jnp.arange(seq) % GRID_W, HEAD_DIM // 2)
    cos2 = jnp.concatenate([cr, cr, cc, cc], axis=-1)
    sin2 = jnp.concatenate([-sr, sr, -sc, sc], axis=-1)
    rep = LANES // HEAD_DIM
    nat = lambda c, s: (jnp.tile(c, (1, rep)), jnp.tile(s, (1, rep)))
    return ((cos1.T, sin1.T), nat(cos1, sin1)), ((cos2.T, sin2.T), nat(cos2, sin2))


def _prep_params(p):
    bf = lambda w: w.astype(BF16)
    tr = lambda w: jnp.swapaxes(w, -1, -2).astype(BF16)
    a, bw, cw = p["a_w_in"], p["b_w_in"], p["c_w_in"]
    out = dict(p)
    out.update(
        a_wqT=tr(a[:, :, 0:MIX_W]), a_wk=bf(a[:, :, MIX_W:2 * MIX_W]),
        a_wvT=tr(a[:, :, 2 * MIX_W:3 * MIX_W]), a_wmT=tr(a[:, :, 3 * MIX_W:]),
        w_memk=bf(p["w_mem_kv"][:, :, 0:MEM_W]), w_memvT=tr(p["w_mem_kv"][:, :, MEM_W:]),
        w_o=bf(p["w_o"]), w_routerT=tr(p["w_router"]),
    )
    for name, w in (("b", bw), ("c", cw)):
        out[name + "_wqT"] = tr(w[:, :, 0:MIX_W])
        out[name + "_wk"] = bf(w[:, :, MIX_W:MIX_W + KV_W])
        out[name + "_wvT"] = tr(w[:, :, MIX_W + KV_W:MIX_W + 2 * KV_W])
        out[name + "_wmT"] = tr(w[:, :, MIX_W + 2 * KV_W:])
    return out


def _encode(x, mem, p):
    b, s, d = x.shape
    n = b * s
    m_len = mem.shape[1]
    depth = p["w_o"].shape[0]
    alpha = (2 * depth) ** 0.25
    (tabs1T, tabs1), (tabs2T, tabs2) = _rope_tables(s)
    bd = (jnp.arange(LANES)[:, None] // HEAD_DIM == jnp.arange(LANES)[None, :] // HEAD_DIM).astype(BF16)
    ones_g = jnp.ones((1, LANES), F32)
    ones_gT = jnp.ones((HEAD_DIM, LANES), F32)
    cap = EC_FACTOR * n // N_EXPERTS
    x2d = x.reshape(n, d)
    mem2d = mem.reshape(b * m_len, d)
    tq_full, tk_full = 512, 512

    for i in range(depth):
        kind, j = i % N_MIXERS, i // N_MIXERS
        common = dict(batch=b, seq=s)
        if kind == 0:
            qT, k4, vT5, qmT = _inproj(x2d, p["a_wqT"][j], p["a_wvT"][j], p["a_wmT"][j], p["a_wk"][j],
                                       tabs1T, tabs1, ones_gT, ones_g, bd, half=HEAD_DIM // 2,
                                       qk_norm=False, vchunk=tk_full, **common)
            lam_init = 0.8 - 0.6 * math.exp(-0.3 * i)
            tk = min(tk_full, s)
            k5 = k4.reshape(b, 2 * A_HEADS, s // tk, tk, HEAD_DIM)
            g_col = jnp.broadcast_to(p["a_subln"][j][:, None], (A_V_DIM, LANES))
            mixT = _diff_attention(qT, k5, vT5, p["a_lambda"][j], g_col, lam_init, 2 * tq_full)
        elif kind == 1:
            qT, k4, vT5, qmT = _inproj(x2d, p["b_wqT"][j], p["b_wvT"][j], p["b_wmT"][j], p["b_wk"][j],
                                       tabs1T, tabs1, ones_gT, ones_g, bd, half=HEAD_DIM // 2,
                                       qk_norm=False, vchunk=LANES, **common)
            k5 = k4.reshape(b, GQA_KV_HEADS, s // LANES, LANES, HEAD_DIM)
            mixT = _window_attention(qT, k5, vT5, p["b_sink"][j], GQA_GROUP, tq_full)
        else:
            gq = jnp.broadcast_to(p["c_qk_norm"][j][0][:, None], (HEAD_DIM, LANES))
            gk = jnp.tile(p["c_qk_norm"][j][1], LANES // HEAD_DIM)[None, :]
            qT, k4, vT5, qmT = _inproj(x2d, p["c_wqT"][j], p["c_wvT"][j], p["c_wmT"][j], p["c_wk"][j],
                                       tabs2T, tabs2, gq, gk, bd, half=HEAD_DIM // 4,
                                       qk_norm=True, vchunk=tk_full // 2, **common)
            tk = min(tk_full // 2, s)
            k5 = k4.reshape(b, GQA_KV_HEADS, s // tk, tk, HEAD_DIM)
            mixT = _gqa_attention(qT, k5, vT5, GQA_GROUP, 2 * tq_full)

        km, vmT = _memproj(mem2d, p["w_memk"][i], p["w_memvT"][i], b, m_len)
        memT = _gqa_attention(qmT, km.reshape(b, MEM_HEADS, 1, m_len, HEAD_DIM), vmT, 1, 2 * tq_full)

        x2d, affT = _oproj(mixT, memT, x2d, p["w_o"][i], p["ln_mix"][i], p["w_routerT"][i], alpha, b, s)

        affT3 = affT.reshape(N_EXPERTS, n // LANES, LANES)
        posm, off = _route_select(affT3, cap)
        idx, gates = _route_compact(off, posm, affT3, cap)
        y = _expert_ffn(idx, x2d, p["w_gate_up"], p["w_down"], i, gates)
        x2d = _combine_ln(off, x2d, posm, y, p["ln_ffn"][i], alpha, cap)
    return x2d.reshape(b, s, d)


def kernel(x_prompt, x_sample, mem_prompt, mem_sample, a_w_in, a_lambda, a_subln, b_w_in, b_sink, c_w_in, c_qk_norm, w_mem_kv, w_o, ln_mix, w_router, w_gate_up, w_down, ln_ffn):
    p = _prep_params(dict(
        a_w_in=a_w_in, a_lambda=a_lambda, a_subln=a_subln, b_w_in=b_w_in, b_sink=b_sink,
        c_w_in=c_w_in, c_qk_norm=c_qk_norm, w_mem_kv=w_mem_kv, w_o=w_o, ln_mix=ln_mix,
        w_router=w_router, w_gate_up=w_gate_up, w_down=w_down, ln_ffn=ln_ffn))
    return (_encode(x_prompt, mem_prompt, p), _encode(x_sample, mem_sample, p))
```

```python
import functools
import math

import jax
import jax.numpy as jnp
from jax import lax
from jax.experimental import pallas as pl
from jax.experimental.pallas import tpu as pltpu

F32 = jnp.float32
BF16 = jnp.bfloat16

HEAD_DIM = 64
GQA_KV_HEADS = 4
GQA_GROUP = 3
A_HEADS = 6
A_V_DIM = 2 * HEAD_DIM
MEM_HEADS = 4
MIX_W = 768
MEM_W = MEM_HEADS * HEAD_DIM
KV_W = GQA_KV_HEADS * HEAD_DIM
N_MIXERS = 3
WINDOW = 128
GRID_W = 64
ROPE_THETA = 10000.0
N_EXPERTS = 16
EC_FACTOR = 2
LN_EPS = 1e-5
RMS_EPS = 1e-6
LOG2E = 1.4426950408889634
QK_SCALE = HEAD_DIM ** -0.5 * LOG2E
LANES = 128
ONES_ROWS = 16
ATTN_UNROLL = 4
VMEM_LIMIT = 52 * 1024 * 1024

_NT = (((1,), (1,)), ((), ()))
_TN = (((0,), (0,)), ((), ()))


def _cparams(grid_rank):
    return pltpu.CompilerParams(dimension_semantics=("arbitrary",) * grid_rank,
                                vmem_limit_bytes=VMEM_LIMIT)


def _layer_norm(h, g, b):
    mu = jnp.mean(h, axis=-1, keepdims=True)
    d = h - mu
    var = jnp.mean(d * d, axis=-1, keepdims=True)
    return d * lax.rsqrt(var + LN_EPS) * g + b


def _lane_tile(a, width):
    return jnp.concatenate([a] * (width // a.shape[1]), axis=1)


def _inproj_kernel(x_ref, wqT_ref, wvT_ref, wmT_ref, wk_ref, cosT_ref, sinT_ref, cos_ref, sin_ref,
                   gT_ref, g_ref, bd_ref, qT_ref, k_ref, vT_ref, qmT_ref, *, half, qk_norm, vchunk):
    xb = x_ref[...].astype(BF16)
    tm = xb.shape[0]
    cosT = cosT_ref[...]
    sinT = sinT_ref[...]
    nparts = HEAD_DIM // half

    yqT = lax.dot_general(wqT_ref[...], xb, _NT, preferred_element_type=F32)
    for h in range(yqT.shape[0] // HEAD_DIM):
        xh = yqT[h * HEAD_DIM:(h + 1) * HEAD_DIM]
        if qk_norm:
            ms = jnp.mean(xh * xh, axis=0, keepdims=True)
            xh = xh * lax.rsqrt(ms + RMS_EPS) * _lane_tile(gT_ref[...], tm)
        rot = jnp.concatenate([xh[(p ^ 1) * half:((p ^ 1) + 1) * half] for p in range(nparts)], axis=0)
        qT_ref[0, h * HEAD_DIM:(h + 1) * HEAD_DIM, :] = ((xh * cosT + rot * sinT) * QK_SCALE).astype(BF16)

    yvT = lax.dot_general(wvT_ref[...], xb, _NT, preferred_element_type=F32).astype(BF16)
    for c in range(tm // vchunk):
        vT_ref[0, c] = yvT[:, c * vchunk:(c + 1) * vchunk]
    ymT = lax.dot_general(wmT_ref[...], xb, _NT, preferred_element_type=F32)
    qmT_ref[0] = (ymT * QK_SCALE).astype(BF16)

    lane = lax.broadcasted_iota(jnp.int32, (tm, LANES), 1)
    first = (lane % (2 * half)) < half
    cos = cos_ref[...]
    sin = sin_ref[...]
    yk = jnp.dot(xb, wk_ref[...], preferred_element_type=F32)
    for c in range(yk.shape[1] // LANES):
        ch = yk[:, c * LANES:(c + 1) * LANES]
        if qk_norm:
            sq = ch * ch
            hi = sq.astype(BF16)
            lo = (sq - hi.astype(F32)).astype(BF16)
            ss = (jnp.dot(hi, bd_ref[...], preferred_element_type=F32)
                  + jnp.dot(lo, bd_ref[...], preferred_element_type=F32))
            ch = ch * lax.rsqrt(ss * (1.0 / HEAD_DIM) + RMS_EPS) * g_ref[...]
        rot = jnp.where(first, pltpu.roll(ch, LANES - half, 1), pltpu.roll(ch, half, 1))
        ch = (ch * cos + rot * sin).astype(BF16)
        k_ref[0, 2 * c] = ch[:, 0:HEAD_DIM]
        k_ref[0, 2 * c + 1] = ch[:, HEAD_DIM:LANES]


def _inproj(x2d, wqT, wvT, wmT, wk, tabsT, tabs, gT, g, bd, *, batch, seq, half, qk_norm, vchunk):
    n, d = x2d.shape
    tm = min(512, seq)
    vchunk = min(vchunk, tm)
    nb = seq // tm
    qw, vw, kw = wqT.shape[0], wvT.shape[0], wk.shape[1]
    kern = functools.partial(_inproj_kernel, half=half, qk_norm=qk_norm, vchunk=vchunk)
    const = lambda b, i: (0, 0)
    return pl.pallas_call(
        kern,
        grid=(batch, nb),
        in_specs=[
            pl.BlockSpec((tm, d), lambda b, i: (b * nb + i, 0)),
            pl.BlockSpec(wqT.shape, const),
            pl.BlockSpec(wvT.shape, const),
            pl.BlockSpec(wmT.shape, const),
            pl.BlockSpec(wk.shape, const),
            pl.BlockSpec((HEAD_DIM, tm), lambda b, i: (0, i)),
            pl.BlockSpec((HEAD_DIM, tm), lambda b, i: (0, i)),
            pl.BlockSpec((tm, LANES), lambda b, i: (i, 0)),
            pl.BlockSpec((tm, LANES), lambda b, i: (i, 0)),
            pl.BlockSpec((HEAD_DIM, LANES), const),
            pl.BlockSpec((1, LANES), const),
            pl.BlockSpec((LANES, LANES), const),
        ],
        out_specs=[
            pl.BlockSpec((1, qw, tm), lambda b, i: (b, 0, i)),
            pl.BlockSpec((1, kw // HEAD_DIM, tm, HEAD_DIM), lambda b, i: (b, 0, i, 0)),
            pl.BlockSpec((1, tm // vchunk, vw, vchunk), lambda b, i: (b, i, 0, 0)),
            pl.BlockSpec((1, MEM_W, tm), lambda b, i: (b, 0, i)),
        ],
        out_shape=[
            jax.ShapeDtypeStruct((batch, qw, seq), BF16),
            jax.ShapeDtypeStruct((batch, kw // HEAD_DIM, seq, HEAD_DIM), BF16),
            jax.ShapeDtypeStruct((batch, seq // vchunk, vw, vchunk), BF16),
            jax.ShapeDtypeStruct((batch, MEM_W, seq), BF16),
        ],
        compiler_params=_cparams(2),
        name="inproj",
    )(x2d, wqT, wvT, wmT, wk, *tabsT, *tabs, gT, g, bd)


def _memproj_kernel(x_ref, wk_ref, wvT_ref, k_ref, vT_ref):
    xb = x_ref[...].astype(BF16)
    yk = jnp.dot(xb, wk_ref[...], preferred_element_type=F32).astype(BF16)
    for h in range(MEM_HEADS):
        k_ref[0, h] = yk[:, h * HEAD_DIM:(h + 1) * HEAD_DIM]
    vT_ref[0, 0] = lax.dot_general(wvT_ref[...], xb, _NT, preferred_element_type=F32).astype(BF16)


def _memproj(mem2d, wk, wvT, batch, m_len):
    d = mem2d.shape[1]
    return pl.pallas_call(
        _memproj_kernel,
        grid=(batch,),
        in_specs=[pl.BlockSpec((m_len, d), lambda b: (b, 0)),
                  pl.BlockSpec(wk.shape, lambda b: (0, 0)),
                  pl.BlockSpec(wvT.shape, lambda b: (0, 0))],
        out_specs=[pl.BlockSpec((1, MEM_HEADS, m_len, HEAD_DIM), lambda b: (b, 0, 0, 0)),
                   pl.BlockSpec((1, 1, MEM_W, m_len), lambda b: (b, 0, 0, 0))],
        out_shape=[jax.ShapeDtypeStruct((batch, MEM_HEADS, m_len, HEAD_DIM), BF16),
                   jax.ShapeDtypeStruct((batch, 1, MEM_W, m_len), BF16)],
        compiler_params=_cparams(1),
        name="memproj",
    )(mem2d, wk, wvT)


def _ones_rows(tk):
    return (lax.broadcasted_iota(jnp.int32, (ONES_ROWS, tk), 0) == 0).astype(BF16)


def _flash_step(sT, va, m, acc):
    m_new = jnp.maximum(m, jnp.max(sT, axis=0, keepdims=True))
    a = jnp.exp2(m - m_new)
    pT = jnp.exp2(sT - m_new).astype(BF16)
    return m_new, a * acc + jnp.dot(va, pT, preferred_element_type=F32)


def _gqa_kernel(qT_ref, k_ref, vT_ref, o_ref, *, group, nk, unroll):
    tk = k_ref.shape[3]
    qT = jnp.concatenate([qT_ref[0, g * HEAD_DIM:(g + 1) * HEAD_DIM, :] for g in range(group)], axis=1)
    nq = qT.shape[1]
    tq = nq // group
    ones = _ones_rows(tk)

    def body(j, carry):
        sT = jnp.dot(k_ref[0, 0, j], qT, preferred_element_type=F32)
        va = jnp.concatenate([vT_ref[0, j], ones], axis=0)
        return _flash_step(sT, va, *carry)

    init = (jnp.full((1, nq), -jnp.inf, F32), jnp.zeros((HEAD_DIM + ONES_ROWS, nq), F32))
    _, acc = lax.fori_loop(0, nk, body, init, unroll=unroll)
    o = acc[0:HEAD_DIM] / acc[HEAD_DIM:HEAD_DIM + 1]
    for g in range(group):
        o_ref[0, g * HEAD_DIM:(g + 1) * HEAD_DIM, :] = o[:, g * tq:(g + 1) * tq].astype(o_ref.dtype)


def _gqa_attention(qT, k5, vT5, group, tq):
    b, qrows, s = qT.shape
    _, kvh, nk, tk, _ = k5.shape
    tq = min(tq, s)
    kern = functools.partial(_gqa_kernel, group=group, nk=nk, unroll=math.gcd(nk, 2 * ATTN_UNROLL))
    return pl.pallas_call(
        kern,
        grid=(b, kvh, s // tq),
        in_specs=[
            pl.BlockSpec((1, group * HEAD_DIM, tq), lambda bi, h, i: (bi, h, i)),
            pl.BlockSpec((1, 1, nk, tk, HEAD_DIM), lambda bi, h, i: (bi, h, 0, 0, 0)),
            pl.BlockSpec((1, nk, HEAD_DIM, tk), lambda bi, h, i: (bi, 0, h, 0)),
        ],
        out_specs=pl.BlockSpec((1, group * HEAD_DIM, tq), lambda bi, h, i: (bi, h, i)),
        out_shape=jax.ShapeDtypeStruct(qT.shape, BF16),
        compiler_params=_cparams(3),
        name="gqa_attention",
    )(qT, k5, vT5)


def _diff_kernel(lam_ref, g_ref, qT_ref, k_ref, vT_ref, o_ref, *, nk, unroll, lam_init):
    tk = k_ref.shape[3]
    tq = qT_ref.shape[2]
    lp = lam_ref[...]
    lam = (jnp.exp(jnp.sum(lp[0:1] * lp[1:2], axis=-1, keepdims=True))
           - jnp.exp(jnp.sum(lp[2:3] * lp[3:4], axis=-1, keepdims=True)) + lam_init)
    q0T = qT_ref[0, 0:HEAD_DIM, :]
    q1T = qT_ref[0, HEAD_DIM:2 * HEAD_DIM, :]
    ones = _ones_rows(tk)

    def body(j, carry):
        sT = jnp.concatenate([jnp.dot(k_ref[0, 0, j], q0T, preferred_element_type=F32),
                              jnp.dot(k_ref[0, 1, j], q1T, preferred_element_type=F32)], axis=1)
        va = jnp.concatenate([vT_ref[0, j], ones], axis=0)
        return _flash_step(sT, va, *carry)

    init = (jnp.full((1, 2 * tq), -jnp.inf, F32), jnp.zeros((A_V_DIM + ONES_ROWS, 2 * tq), F32))
    _, acc = lax.fori_loop(0, nk, body, init, unroll=unroll)
    o = acc[0:A_V_DIM] / acc[A_V_DIM:A_V_DIM + 1]
    o = o[:, 0:tq] - lam * o[:, tq:2 * tq]
    ms = jnp.mean(o * o, axis=0, keepdims=True)
    o = o * lax.rsqrt(ms + RMS_EPS) * _lane_tile(g_ref[...], tq) * (1.0 - lam_init)
    o_ref[0] = o.astype(o_ref.dtype)


def _diff_attention(qT, k5, vT5, lam_p, g_col, lam_init, tq):
    b, qrows, s = qT.shape
    _, _, nk, tk, _ = k5.shape
    tq = min(tq, s)
    kern = functools.partial(_diff_kernel, nk=nk, unroll=math.gcd(nk, 2 * ATTN_UNROLL), lam_init=lam_init)
    return pl.pallas_call(
        kern,
        grid=(b, A_HEADS, s // tq),
        in_specs=[
            pl.BlockSpec((4, HEAD_DIM), lambda bi, h, i: (0, 0)),
            pl.BlockSpec((A_V_DIM, LANES), lambda bi, h, i: (0, 0)),
            pl.BlockSpec((1, 2 * HEAD_DIM, tq), lambda bi, h, i: (bi, h, i)),
            pl.BlockSpec((1, 2, nk, tk, HEAD_DIM), lambda bi, h, i: (bi, h, 0, 0, 0)),
            pl.BlockSpec((1, nk, A_V_DIM, tk), lambda bi, h, i: (bi, 0, h, 0)),
        ],
        out_specs=pl.BlockSpec((1, A_V_DIM, tq), lambda bi, h, i: (bi, h, i)),
        out_shape=jax.ShapeDtypeStruct(qT.shape, BF16),
        compiler_params=_cparams(3),
        name="diff_attention",
    )(lam_p, g_col, qT, k5, vT5)


def _window_kernel(sink_ref, qT_ref, k_ref, vT_ref, o_ref, *, group, nwc, nchunks):
    h = pl.program_id(1)
    tq = qT_ref.shape[2]
    nq = group * tq
    q0 = pl.program_id(2) * tq
    c0 = jnp.clip(q0 // LANES - WINDOW // LANES, 0, nchunks - nwc)
    qT = jnp.concatenate([qT_ref[0, g * HEAD_DIM:(g + 1) * HEAD_DIM, :] for g in range(group)], axis=1)
    qpos = q0 + lax.broadcasted_iota(jnp.int32, (LANES, nq), 1) % tq
    krow = lax.broadcasted_iota(jnp.int32, (LANES, nq), 0)
    sink = jnp.concatenate(
        [jnp.full((1, tq), sink_ref[h * group + g] * LOG2E, F32) for g in range(group)], axis=1)
    ones = _ones_rows(LANES)
    scores = []
    m = sink
    for c in range(nwc):
        sT = jnp.dot(k_ref[0, 0, c0 + c], qT, preferred_element_type=F32)
        kpos = (c0 + c) * LANES + krow
        sT = jnp.where(jnp.abs(kpos - qpos) <= WINDOW, sT, -jnp.inf)
        scores.append(sT)
        m = jnp.maximum(m, jnp.max(sT, axis=0, keepdims=True))
    acc = jnp.zeros((HEAD_DIM + ONES_ROWS, nq), F32)
    for c in range(nwc):
        va = jnp.concatenate([vT_ref[0, c0 + c], ones], axis=0)
        acc = acc + jnp.dot(va, jnp.exp2(scores[c] - m).astype(BF16), preferred_element_type=F32)
    o = acc[0:HEAD_DIM] / (acc[HEAD_DIM:HEAD_DIM + 1] + jnp.exp2(sink - m))
    for g in range(group):
        o_ref[0, g * HEAD_DIM:(g + 1) * HEAD_DIM, :] = o[:, g * tq:(g + 1) * tq].astype(o_ref.dtype)


def _window_attention(qT, k5, vT5, sink, group, tq):
    b, qrows, s = qT.shape
    _, kvh, nchunks, tk, _ = k5.shape
    tq = min(tq, s)
    nwc = min(nchunks, tq // LANES + 2 * (WINDOW // LANES))
    kern = functools.partial(_window_kernel, group=group, nwc=nwc, nchunks=nchunks)
    return pl.pallas_call(
        kern,
        grid=(b, kvh, s // tq),
        in_specs=[
            pl.BlockSpec(memory_space=pltpu.SMEM),
            pl.BlockSpec((1, group * HEAD_DIM, tq), lambda bi, h, i: (bi, h, i)),
            pl.BlockSpec((1, 1, nchunks, tk, HEAD_DIM), lambda bi, h, i: (bi, h, 0, 0, 0)),
            pl.BlockSpec((1, nchunks, HEAD_DIM, tk), lambda bi, h, i: (bi, 0, h, 0)),
        ],
        out_specs=pl.BlockSpec((1, group * HEAD_DIM, tq), lambda bi, h, i: (bi, h, i)),
        out_shape=jax.ShapeDtypeStruct(qT.shape, BF16),
        compiler_params=_cparams(3),
        name="window_attention",
    )(sink, qT, k5, vT5)


def _oproj_kernel(mixT_ref, memT_ref, x_ref, wo_ref, ln_ref, wrT_ref, xo_ref, affT_ref, *, alpha):
    sub = (lax.dot_general(mixT_ref[0], wo_ref[0:MIX_W, :], _TN, preferred_element_type=F32)
           + lax.dot_general(memT_ref[0], wo_ref[MIX_W:MIX_W + MEM_W, :], _TN,
                             preferred_element_type=F32))
    y = _layer_norm(alpha * x_ref[...] + sub, ln_ref[0:1, :], ln_ref[1:2, :])
    xo_ref[...] = y
    logitsT = lax.dot_general(wrT_ref[...], y.astype(BF16), _NT, preferred_element_type=F32)
    e = jnp.exp(logitsT - jnp.max(logitsT, axis=0, keepdims=True))
    affT_ref[...] = e / jnp.sum(e, axis=0, keepdims=True)


def _oproj(mixT, memT, x2d, wo, ln, wrT, alpha, batch, seq):
    n, d = x2d.shape
    tm = min(512, seq)
    nb = seq // tm
    kern = functools.partial(_oproj_kernel, alpha=alpha)
    const = lambda b, i: (0, 0)
    row = lambda b, i: (b * nb + i, 0)
    return pl.pallas_call(
        kern,
        grid=(batch, nb),
        in_specs=[
            pl.BlockSpec((1, MIX_W, tm), lambda b, i: (b, 0, i)),
            pl.BlockSpec((1, MEM_W, tm), lambda b, i: (b, 0, i)),
            pl.BlockSpec((tm, d), row),
            pl.BlockSpec(wo.shape, const),
            pl.BlockSpec((2, d), const),
            pl.BlockSpec(wrT.shape, const),
        ],
        out_specs=[
            pl.BlockSpec((tm, d), row),
            pl.BlockSpec((N_EXPERTS, tm), lambda b, i: (0, b * nb + i)),
        ],
        out_shape=[
            jax.ShapeDtypeStruct((n, d), F32),
            jax.ShapeDtypeStruct((N_EXPERTS, n), F32),
        ],
        compiler_params=_cparams(2),
        name="oproj_ln_router",
    )(mixT, memT, x2d, wo, ln, wrT)


def _route_select_kernel(affT_ref, tri_ref, ones_ref, low_ref, posm_ref, off_ref, *, cap):
    a = affT_ref[0]
    bits = lax.bitcast_convert_type(a, jnp.int32)

    def count(mask):
        return jnp.sum(jnp.sum(mask.astype(F32), axis=1, keepdims=True), axis=0, keepdims=True)

    def bisect(i, t):
        cand = t | jnp.left_shift(jnp.int32(1), 30 - i)
        return jnp.where(count(bits >= cand) >= cap, cand, t)

    t = lax.fori_loop(0, 31, bisect, jnp.zeros((1, 1), jnp.int32))

    def excl_cumsum(mask):
        mb = mask.astype(BF16)
        incl = jnp.dot(mb, tri_ref[...], preferred_element_type=F32)
        tot = jnp.dot(mb, ones_ref[...], preferred_element_type=F32)
        before = jnp.dot(low_ref[...], tot.astype(BF16), preferred_element_type=F32)
        return before + incl - mask.astype(F32), before

    gt = bits > t
    eq = bits == t
    need = cap - count(gt)
    rank, _ = excl_cumsum(eq)
    sel = gt | (eq & (rank < need))
    pos, before = excl_cumsum(sel)
    posm_ref[0] = jnp.where(sel, pos, -1.0).astype(jnp.int32)
    off_ref[0] = before[:, 0:1].astype(jnp.int32)


def _route_select(affT3, cap):
    e, r, _ = affT3.shape
    tri = (jnp.arange(LANES)[:, None] <= jnp.arange(LANES)[None, :]).astype(BF16)
    ones = jnp.ones((LANES, LANES), BF16)
    low = (jnp.arange(r)[None, :] < jnp.arange(r)[:, None]).astype(BF16)
    posm, off = pl.pallas_call(
        functools.partial(_route_select_kernel, cap=cap),
        grid=(e,),
        in_specs=[pl.BlockSpec((1, r, LANES), lambda ei: (ei, 0, 0)),
                  pl.BlockSpec((LANES, LANES), lambda ei: (0, 0)),
                  pl.BlockSpec((LANES, LANES), lambda ei: (0, 0)),
                  pl.BlockSpec((r, r), lambda ei: (0, 0))],
        out_specs=[pl.BlockSpec((1, r, LANES), lambda ei: (ei, 0, 0)),
                   pl.BlockSpec((1, r, 1), lambda ei: (ei, 0, 0))],
        out_shape=[jax.ShapeDtypeStruct((e, r, LANES), jnp.int32),
                   jax.ShapeDtypeStruct((e, r, 1), jnp.int32)],
        compiler_params=_cparams(1),
        name="route_select",
    )(affT3, tri, ones, low)
    return posm, off.reshape(e, r)


COMPACT_ROWS = 16


def _route_compact_kernel(off_ref, posm_ref, affT_ref, idx_ref, gate_ref, acc_ref, *, nrows, ncb):
    e = pl.program_id(0)
    acc_ref[...] = jnp.zeros_like(acc_ref)
    slot = lax.broadcasted_iota(jnp.int32, (LANES, LANES), 0)
    sub = lax.broadcasted_iota(jnp.int32, (COMPACT_ROWS, LANES), 0)
    lane = lax.broadcasted_iota(jnp.int32, (COMPACT_ROWS, LANES), 1).astype(F32)

    def body(r, carry):
        cb = lax.shift_right_logical(off_ref[e, r], 7)
        rel = posm_ref[0, pl.ds(r, 1), :] - cb * LANES
        a = affT_ref[0, pl.ds(r, 1), :]
        hi = a.astype(BF16).astype(F32)
        mid = (a - hi).astype(BF16).astype(F32)
        lo = a - hi - mid
        vals = jnp.where(sub == 0, lane, jnp.where(sub == 1, jnp.asarray(r, F32), jnp.where(
            sub == 2, hi, jnp.where(sub == 3, mid, jnp.where(sub == 4, lo, 0.0))))).astype(BF16)
        for part in range(2):
            hit = ((rel - part * LANES) == slot).astype(BF16)
            acc_ref[cb + part] += lax.dot_general(vals, hit, _NT, preferred_element_type=F32)
        return carry

    lax.fori_loop(0, nrows, body, 0, unroll=16)
    acc = acc_ref[0:ncb]
    idx_ref[0] = (acc[:, 1:2, :] * LANES + acc[:, 0:1, :]).astype(jnp.int32)
    gate_ref[0] = acc[:, 2:3, :] + acc[:, 3:4, :] + acc[:, 4:5, :]


def _route_compact(off, posm, affT3, cap):
    e, r, _ = posm.shape
    ncb = cap // LANES
    idx, gate = pl.pallas_call(
        functools.partial(_route_compact_kernel, nrows=r, ncb=ncb),
        grid=(e,),
        in_specs=[pl.BlockSpec(memory_space=pltpu.SMEM),
                  pl.BlockSpec((1, r, LANES), lambda ei: (ei, 0, 0)),
                  pl.BlockSpec((1, r, LANES), lambda ei: (ei, 0, 0))],
        out_specs=[pl.BlockSpec((1, ncb, 1, LANES), lambda ei: (ei, 0, 0, 0)),
                   pl.BlockSpec((1, ncb, 1, LANES), lambda ei: (ei, 0, 0, 0))],
        out_shape=[jax.ShapeDtypeStruct((e, ncb, 1, LANES), jnp.int32),
                   jax.ShapeDtypeStruct((e, ncb, 1, LANES), F32)],
        scratch_shapes=[pltpu.VMEM((ncb + 2, COMPACT_ROWS, LANES), F32)],
        compiler_params=_cparams(1),
        name="route_compact",
    )(off, posm, affT3)
    return idx.reshape(e, cap), gate.reshape(e, cap, 1)


FFN_ROWS = 512


def _ffn_kernel(idx_ref, nxt_ref, x_hbm, wg_ref, wu_ref, wd_ref, gate_ref, y_ref,
                xf_ref, xb_ref, acc_ref, sem, *, nf, nblk):
    f = pl.program_id(2)
    blk = pl.program_id(0) * pl.num_programs(1) + pl.program_id(1)
    slot = blk % 2
    tc = xb_ref.shape[0]
    tcp = xf_ref.shape[1]
    per_step = tcp // nf
    chunk = min(FFN_ROWS, tc)

    def row_copy(ids, i, s):
        return pltpu.make_async_copy(x_hbm.at[pl.ds(ids[0, 0, i], 1)], xf_ref.at[s, pl.ds(i, 1)], sem.at[s])

    def all_rows(ids, s, fn):
        def one(i, c):
            fn(row_copy(ids, i, s))
            return c
        lax.fori_loop(0, tcp, one, 0, unroll=8)

    @pl.when(f == 0)
    def _():
        @pl.when(blk == 0)
        def _():
            all_rows(idx_ref, slot, lambda c: c.start())

        all_rows(idx_ref, slot, lambda c: c.wait())
        for c in range(tc // chunk):
            rows = slice(c * chunk, (c + 1) * chunk)
            xb_ref[rows, :] = xf_ref[slot, rows, :].astype(BF16)
        acc_ref[...] = jnp.zeros_like(acc_ref)

    wg = wg_ref[0, 0].astype(BF16)
    wu = wu_ref[0, 0].astype(BF16)
    wd = wd_ref[0, 0].astype(BF16)
    for c in range(tc // chunk):
        rows = slice(c * chunk, (c + 1) * chunk)
        x = xb_ref[rows, :]
        g = jnp.dot(x, wg, preferred_element_type=F32)
        u = jnp.dot(x, wu, preferred_element_type=F32)
        h = (g / (1.0 + jnp.exp(-g))) * u
        acc_ref[rows, :] += jnp.dot(h.astype(BF16), wd, preferred_element_type=F32)

    for j in range(per_step):
        row_copy(nxt_ref, f * per_step + j, 1 - slot).start()

    @pl.when(f == nf - 1)
    def _():
        y_ref[0] = (acc_ref[...] * gate_ref[0]).astype(y_ref.dtype)

        @pl.when(blk == nblk - 1)
        def _():
            all_rows(nxt_ref, 1 - slot, lambda c: c.wait())


def _expert_ffn(idx, x2d, w_gu, w_dn, layer, gates):
    e, c = idx.shape
    d = x2d.shape[1]
    fdim = w_dn.shape[2]
    tc = min(2048, c)
    ncb = c // tc
    nblk = e * ncb
    tf = 256
    nf = fdim // tf
    assert tc % min(FFN_ROWS, tc) == 0
    per_step = -(-tc // nf)
    per_step += -per_step % 8
    tcp = per_step * nf
    kern = functools.partial(_ffn_kernel, nf=nf, nblk=nblk)
    idx3 = jnp.pad(idx.reshape(nblk, 1, tc), ((0, 0), (0, 0), (0, tcp - tc)))
    return pl.pallas_call(
        kern,
        grid=(e, ncb, nf),
        in_specs=[
            pl.BlockSpec((1, 1, tcp), lambda ei, ci, fi: (ei * ncb + ci, 0, 0), memory_space=pltpu.SMEM),
            pl.BlockSpec((1, 1, tcp), lambda ei, ci, fi: (jnp.minimum(ei * ncb + ci + 1, nblk - 1), 0, 0),
                         memory_space=pltpu.SMEM),
            pl.BlockSpec(memory_space=pl.ANY),
            pl.BlockSpec((1, 1, d, tf), lambda ei, ci, fi: (layer, ei, 0, fi)),
            pl.BlockSpec((1, 1, d, tf), lambda ei, ci, fi: (layer, ei, 0, fi + nf)),
            pl.BlockSpec((1, 1, tf, d), lambda ei, ci, fi: (layer, ei, fi, 0)),
            pl.BlockSpec((1, tc, 1), lambda ei, ci, fi: (ei, ci, 0)),
        ],
        out_specs=pl.BlockSpec((1, tc, d), lambda ei, ci, fi: (ei, ci, 0)),
        out_shape=jax.ShapeDtypeStruct((e, c, d), BF16),
        scratch_shapes=[pltpu.VMEM((2, tcp, d), F32), pltpu.VMEM((tc, d), BF16), pltpu.VMEM((tc, d), F32),
                        pltpu.SemaphoreType.DMA((2,))],
        compiler_params=_cparams(3),
        name="expert_ffn",
    )(idx3, idx3, x2d, w_gu, w_gu, w_dn, gates)


COMBINE_ROWS = 8
BF16_ROWS = 16
WIN = LANES + BF16_ROWS
WIN_SHORT = 3 * BF16_ROWS


def _combine_kernel(off_ref, x_ref, posm_ref, y_hbm, ln_ref, o_ref, ybuf, ysh, sem, *, alpha, cap):
    t = pl.program_id(0)
    n_exp = posm_ref.shape[0]
    d = x_ref.shape[1]

    def plan(rr):
        starts = []
        short = None
        for e in range(n_exp):
            start = lax.shift_right_logical(off_ref[e, t * COMBINE_ROWS + rr], 4) * BF16_ROWS
            start = pl.multiple_of(jnp.minimum(start, cap - WIN), BF16_ROWS)
            fits = off_ref[e, t * COMBINE_ROWS + rr + 1] - start <= WIN_SHORT
            short = fits if short is None else short & fits
            starts.append(start)
        return starts, short

    def either(rr, par, fn):
        starts, short = plan(rr)
        for cond, nrows, buf in ((short, WIN_SHORT, ysh), (jnp.logical_not(short), WIN, ybuf)):
            @pl.when(cond)
            def _():
                copies = [pltpu.make_async_copy(y_hbm.at[e, pl.ds(starts[e], nrows)], buf.at[par, e], sem.at[par])
                          for e in range(n_exp)]
                fn(starts, nrows, buf, copies)

    def start_row(starts, nrows, buf, copies):
        for e, c in enumerate(copies):
            c.start(priority=e % 2)

    def row(rr, par):
        @pl.when(rr + 1 < COMBINE_ROWS)
        def _():
            either(rr + 1, 1 - par, start_row)

        def finish(starts, nrows, buf, copies):
            for c in copies:
                c.wait()
            slot = lax.broadcasted_iota(jnp.int32, (nrows, LANES), 0)
            hit = jnp.concatenate(
                [((posm_ref[e, pl.ds(rr, 1), :] - starts[e]) == slot).astype(BF16) for e in range(n_exp)],
                axis=0)
            tot = lax.dot_general(hit, buf[par].reshape(n_exp * nrows, d), _TN, preferred_element_type=F32)
            rows = pl.ds(pl.multiple_of(rr * LANES, LANES), LANES)
            o_ref[rows, :] = _layer_norm(alpha * x_ref[rows, :] + tot, ln_ref[0:1, :], ln_ref[1:2, :])

        either(rr, par, finish)

    either(0, 0, start_row)

    def body(k, carry):
        row(2 * k, 0)
        row(2 * k + 1, 1)
        return carry

    lax.fori_loop(0, COMBINE_ROWS // 2, body, 0)


def _combine_ln(off, x2d, posm, y, ln, alpha, cap):
    n, d = x2d.shape
    e, r, _ = posm.shape
    rows = COMBINE_ROWS
    assert r % rows == 0 and rows % 2 == 0 and cap >= WIN
    tt = rows * LANES
    off = jnp.concatenate([off, jnp.full((e, 1), cap, jnp.int32)], axis=1)
    return pl.pallas_call(
        functools.partial(_combine_kernel, alpha=alpha, cap=cap),
        grid=(n // tt,),
        in_specs=[pl.BlockSpec(memory_space=pltpu.SMEM),
                  pl.BlockSpec((tt, d), lambda ti: (ti, 0)),
                  pl.BlockSpec((e, rows, LANES), lambda ti: (0, ti, 0)),
                  pl.BlockSpec(memory_space=pl.ANY),
                  pl.BlockSpec((2, d), lambda ti: (0, 0))],
        out_specs=pl.BlockSpec((tt, d), lambda ti: (ti, 0)),
        out_shape=jax.ShapeDtypeStruct((n, d), F32),
        scratch_shapes=[pltpu.VMEM((2, e, WIN, d), BF16), pltpu.VMEM((2, e, WIN_SHORT, d), BF16),
                        pltpu.SemaphoreType.DMA((2,))],
        compiler_params=_cparams(1),
        name="combine_ln",
    )(off, x2d, posm, y, ln)


def _rope_tables(seq):
    def tab(pos, dim):
        inv = ROPE_THETA ** (-jnp.arange(0, dim, 2, dtype=F32) / dim)
        ang = pos.astype(F32)[:, None] * inv[None, :]
        return jnp.cos(ang), jnp.sin(ang)

    c1, s1 = tab(jnp.arange(seq), HEAD_DIM)
    cos1 = jnp.concatenate([c1, c1], axis=-1)
    sin1 = jnp.concatenate([-s1, s1], axis=-1)
    cr, sr = tab(jnp.arange(seq) // GRID_W, HEAD_DIM // 2)
    cc, sc = tab(jnp.arange(seq) % GRID_W, HEAD_DIM // 2)
    cos2 = jnp.concatenate([cr, cr, cc, cc], axis=-1)
    sin2 = jnp.concatenate([-sr, sr, -sc, sc], axis=-1)
    rep = LANES // HEAD_DIM
    nat = lambda c, s: (jnp.tile(c, (1, rep)), jnp.tile(s, (1, rep)))
    return ((cos1.T, sin1.T), nat(cos1, sin1)), ((cos2.T, sin2.T), nat(cos2, sin2))


def _prep_params(p):
    bf = lambda w: w.astype(BF16)
    tr = lambda w: jnp.swapaxes(w, -1, -2).astype(BF16)
    a, bw, cw = p["a_w_in"], p["b_w_in"], p["c_w_in"]
    out = dict(p)
    out.update(
        a_wqT=tr(a[:, :, 0:MIX_W]), a_wk=bf(a[:, :, MIX_W:2 * MIX_W]),
        a_wvT=tr(a[:, :, 2 * MIX_W:3 * MIX_W]), a_wmT=tr(a[:, :, 3 * MIX_W:]),
        w_memk=bf(p["w_mem_kv"][:, :, 0:MEM_W]), w_memvT=tr(p["w_mem_kv"][:, :, MEM_W:]),
        w_o=bf(p["w_o"]), w_routerT=tr(p["w_router"]),
    )
    for name, w in (("b", bw), ("c", cw)):
        out[name + "_wqT"] = tr(w[:, :, 0:MIX_W])
        out[name + "_wk"] = bf(w[:, :, MIX_W:MIX_W + KV_W])
        out[name + "_wvT"] = tr(w[:, :, MIX_W + KV_W:MIX_W + 2 * KV_W])
        out[name + "_wmT"] = tr(w[:, :, MIX_W + 2 * KV_W:])
    return out


def _encode(x, mem, p):
    b, s, d = x.shape
    n = b * s
    m_len = mem.shape[1]
    depth = p["w_o"].shape[0]
    alpha = (2 * depth) ** 0.25
    (tabs1T, tabs1), (tabs2T, tabs2) = _rope_tables(s)
    bd = (jnp.arange(LANES)[:, None] // HEAD_DIM == jnp.arange(LANES)[None, :] // HEAD_DIM).astype(BF16)
    ones_g = jnp.ones((1, LANES), F32)
    ones_gT = jnp.ones((HEAD_DIM, LANES), F32)
    cap = EC_FACTOR * n // N_EXPERTS
    x2d = x.reshape(n, d)
    mem2d = mem.reshape(b * m_len, d)
    tq_full, tk_full = 512, 512

    for i in range(depth):
        kind, j = i % N_MIXERS, i // N_MIXERS
        common = dict(batch=b, seq=s)
        if kind == 0:
            qT, k4, vT5, qmT = _inproj(x2d, p["a_wqT"][j], p["a_wvT"][j], p["a_wmT"][j], p["a_wk"][j],
                                       tabs1T, tabs1, ones_gT, ones_g, bd, half=HEAD_DIM // 2,
                                       qk_norm=False, vchunk=tk_full, **common)
            lam_init = 0.8 - 0.6 * math.exp(-0.3 * i)
            tk = min(tk_full, s)
            k5 = k4.reshape(b, 2 * A_HEADS, s // tk, tk, HEAD_DIM)
            g_col = jnp.broadcast_to(p["a_subln"][j][:, None], (A_V_DIM, LANES))
            mixT = _diff_attention(qT, k5, vT5, p["a_lambda"][j], g_col, lam_init, 2 * tq_full)
        elif kind == 1:
            qT, k4, vT5, qmT = _inproj(x2d, p["b_wqT"][j], p["b_wvT"][j], p["b_wmT"][j], p["b_wk"][j],
                                       tabs1T, tabs1, ones_gT, ones_g, bd, half=HEAD_DIM // 2,
                                       qk_norm=False, vchunk=LANES, **common)
            k5 = k4.reshape(b, GQA_KV_HEADS, s // LANES, LANES, HEAD_DIM)
            mixT = _window_attention(qT, k5, vT5, p["b_sink"][j], GQA_GROUP, tq_full)
        else:
            gq = jnp.broadcast_to(p["c_qk_norm"][j][0][:, None], (HEAD_DIM, LANES))
            gk = jnp.tile(p["c_qk_norm"][j][1], LANES // HEAD_DIM)[None, :]
            qT, k4, vT5, qmT = _inproj(x2d, p["c_wqT"][j], p["c_wvT"][j], p["c_wmT"][j], p["c_wk"][j],
                                       tabs2T, tabs2, gq, gk, bd, half=HEAD_DIM // 4,
                                       qk_norm=True, vchunk=tk_full // 2, **common)
            tk = min(tk_full // 2, s)
            k5 = k4.reshape(b, GQA_KV_HEADS, s // tk, tk, HEAD_DIM)
            mixT = _gqa_attention(qT, k5, vT5, GQA_GROUP, 2 * tq_full)

        km, vmT = _memproj(mem2d, p["w_memk"][i], p["w_memvT"][i], b, m_len)
        memT = _gqa_attention(qmT, km.reshape(b, MEM_HEADS, 1, m_len, HEAD_DIM), vmT, 1, 2 * tq_full)

        x2d, affT = _oproj(mixT, memT, x2d, p["w_o"][i], p["ln_mix"][i], p["w_routerT"][i], alpha, b, s)

        affT3 = affT.reshape(N_EXPERTS, n // LANES, LANES)
        posm, off = _route_select(affT3, cap)
        idx, gates = _route_compact(off, posm, affT3, cap)
        y = _expert_ffn(idx, x2d, p["w_gate_up"], p["w_down"], i, gates)
        x2d = _combine_ln(off, x2d, posm, y, p["ln_ffn"][i], alpha, cap)
    return x2d.reshape(b, s, d)


def kernel(x_prompt, x_sample, mem_prompt, mem_sample, a_w_in, a_lambda, a_subln, b_w_in, b_sink, c_w_in, c_qk_norm, w_mem_kv, w_o, ln_mix, w_router, w_gate_up, w_down, ln_ffn):
    p = _prep_params(dict(
        a_w_in=a_w_in, a_lambda=a_lambda, a_subln=a_subln, b_w_in=b_w_in, b_sink=b_sink,
        c_w_in=c_w_in, c_qk_norm=c_qk_norm, w_mem_kv=w_mem_kv, w_o=w_o, ln_mix=ln_mix,
        w_router=w_router, w_gate_up=w_gate_up, w_down=w_down, ln_ffn=ln_ffn))
    return (_encode(x_prompt, mem_prompt, p), _encode(x_sample, mem_sample, p))
```
